```python
import jax, jax.numpy as jnp
from jax import lax
import numpy as np

D_MODEL = 2048
BATCH = 16
SEQ = 2048
DEPTH = 1
DEC_BATCH = 1
DEC_SEQ = 8192
PAST_LEN = 128

ATTN_WIDTH = D_MODEL // 2
FOURIER_WIDTH = D_MODEL - ATTN_WIDTH
HEAD_DIM = 128
N_ATTN_HEADS = ATTN_WIDTH // HEAD_DIM
N_FOURIER_GROUPS = 4
FOURIER_GROUP_DIM = FOURIER_WIDTH // N_FOURIER_GROUPS
D_FF = 4 * D_MODEL
DILATED_BRANCHES = ((128, 1), (512, 4), (2048, 16))
N_MOD = 6
RMS_EPS = 1e-6
NEG_INF = -1e30

kernel_name = "hybrid_dilated_attn_fourier_encoder"


def _rms(x, g):
    xf = x.astype(jnp.float32)
    y = xf * lax.rsqrt(jnp.mean(xf * xf, axis=-1, keepdims=True) + RMS_EPS)
    return (y * g.astype(jnp.float32)).astype(x.dtype)


def _alibi_slopes(n_heads):
    return 2.0 ** (-8.0 * (jnp.arange(n_heads, dtype=jnp.float32) + 1.0) / n_heads)


def _band_attention(q, k, v, slopes, half):
    G, L, H, Dh = q.shape
    w = half
    nb = -(-L // w)
    Lp = nb * w
    pad = Lp - L
    qb = jnp.pad(q, ((0, 0), (0, pad), (0, 0), (0, 0))).reshape(G, nb, w, H, Dh)
    kp = jnp.pad(k, ((0, 0), (w, pad + w), (0, 0), (0, 0))).reshape(G, nb + 2, w, H, Dh)
    vp = jnp.pad(v, ((0, 0), (w, pad + w), (0, 0), (0, 0))).reshape(G, nb + 2, w, H, Dh)
    kwin = jnp.concatenate([kp[:, :-2], kp[:, 1:-1], kp[:, 2:]], axis=2)
    vwin = jnp.concatenate([vp[:, :-2], vp[:, 1:-1], vp[:, 2:]], axis=2)
    s = jnp.einsum('gnqhd,gnkhd->gnhqk', qb, kwin).astype(jnp.float32)
    qpos = jnp.arange(Lp).reshape(nb, w)
    kpos = jnp.arange(nb)[:, None] * w - w + jnp.arange(3 * w)[None, :]
    absdist = jnp.abs(qpos[:, :, None] - kpos[:, None, :])
    valid = (absdist <= w) & (kpos >= 0)[:, None, :] & (kpos < L)[:, None, :]
    s = s - slopes[None, None, :, None, None] * absdist[None, :, None].astype(jnp.float32)
    s = jnp.where(valid[None, :, None], s, NEG_INF)
    m = jnp.max(s, axis=-1, keepdims=True)
    p = jnp.exp(s - m)
    den = jnp.sum(p, axis=-1, keepdims=True)
    o = jnp.einsum('gnhqk,gnkhd->gnqhd', p, vwin.astype(jnp.float32))
    o = o / jnp.transpose(den, (0, 1, 3, 2, 4))
    lse = jnp.transpose((m + jnp.log(den))[..., 0], (0, 1, 3, 2))
    o = o.reshape(G, Lp, H, Dh)[:, :L]
    lse = lse.reshape(G, Lp, H)[:, :L]
    return o, lse


def _dilated_branch(q, k, v, slopes, window, dil):
    B, S, H, Dh = q.shape
    L = S // dil

    def split(t):
        return t.reshape(B, L, dil, H, Dh).transpose(0, 2, 1, 3, 4).reshape(B * dil, L, H, Dh)

    o, lse = _band_attention(split(q), split(k), split(v), slopes * dil, window // (2 * dil))
    o = o.reshape(B, dil, L, H, Dh).transpose(0, 2, 1, 3, 4).reshape(B, S, H, Dh)
    lse = lse.reshape(B, dil, L, H).transpose(0, 2, 1, 3).reshape(B, S, H)
    return o, lse


def _dilated_attention(q, k, v):
    slopes = _alibi_slopes(q.shape[2])
    outs, lses = [], []
    for window, dil in DILATED_BRANCHES:
        o, l = _dilated_branch(q, k, v, slopes, window, dil)
        outs.append(o)
        lses.append(l)
    wts = jax.nn.softmax(jnp.stack(lses, axis=0), axis=0)
    out = jnp.sum(wts[..., None] * jnp.stack(outs, axis=0), axis=0)
    return out


def _fourier_mix(u, w_four):
    B, S, _ = u.shape
    ug = u.astype(jnp.float32).reshape(B, S, N_FOURIER_GROUPS, FOURIER_GROUP_DIM)
    f = jnp.fft.fft2(ug, axes=(1, 3), norm='ortho').real
    y = jnp.einsum('bsgc,gce->bsge', f, w_four.astype(jnp.float32))
    return y.reshape(B, S, FOURIER_WIDTH).astype(u.dtype)


def _layer(x, c, w_ada, b_ada, g_pre_mix, w_in, g_attn_out, w_fourier, g_fourier_out,
           w_out, g_post_mix, g_pre_mlp, w_mlp_in, w_mlp_out, g_post_mlp):
    B, S, _ = x.shape
    mod = jnp.matmul(jax.nn.silu(c), w_ada) + b_ada
    sh1, sc1, gt1, sh2, sc2, gt2 = jnp.split(mod[:, None, :], N_MOD, axis=-1)

    h = _rms(x, g_pre_mix) * (1.0 + sc1) + sh1
    z = jnp.matmul(h, w_in)
    q = z[..., :ATTN_WIDTH].reshape(B, S, N_ATTN_HEADS, HEAD_DIM) * (HEAD_DIM ** -0.5)
    k = z[..., ATTN_WIDTH:2 * ATTN_WIDTH].reshape(B, S, N_ATTN_HEADS, HEAD_DIM)
    v = z[..., 2 * ATTN_WIDTH:3 * ATTN_WIDTH].reshape(B, S, N_ATTN_HEADS, HEAD_DIM)
    u = z[..., 3 * ATTN_WIDTH:]
    a = _dilated_attention(q, k, v).reshape(B, S, ATTN_WIDTH).astype(x.dtype)
    f = _fourier_mix(u, w_fourier)
    mixed = jnp.concatenate([_rms(a, g_attn_out), _rms(f, g_fourier_out)], axis=-1)
    y = jnp.matmul(mixed, w_out)
    x = x + gt1 * _rms(y, g_post_mix)

    h = _rms(x, g_pre_mlp) * (1.0 + sc2) + sh2
    hid = jnp.square(jax.nn.relu(jnp.matmul(h, w_mlp_in)))
    y = jnp.matmul(hid, w_mlp_out)
    x = x + gt2 * _rms(y, g_post_mlp)
    return x


def _trunk(x, c, w_ada, b_ada, g_pre_mix, w_in, g_attn_out, w_fourier, g_fourier_out,
           w_out, g_post_mix, g_pre_mlp, w_mlp_in, w_mlp_out, g_post_mlp):
    for l in range(DEPTH):
        x = _layer(x, c, w_ada[l], b_ada[l], g_pre_mix[l], w_in[l], g_attn_out[l],
                   w_fourier[l], g_fourier_out[l], w_out[l], g_post_mix[l], g_pre_mlp[l],
                   w_mlp_in[l], w_mlp_out[l], g_post_mlp[l])
    return x


def setup_inputs(seed: int = 0) -> dict:
    key = jax.random.key(seed)
    ks = jax.random.split(key, 20)
    f32 = jnp.float32

    def nrm(k, shape, scale):
        return jax.random.normal(k, shape, f32) * scale

    def gain(k, n):
        return 1.0 + 0.05 * jax.random.normal(k, (DEPTH, n), f32)

    in_cols = 3 * ATTN_WIDTH + FOURIER_WIDTH
    return {
        "x_prompt": nrm(ks[0], (BATCH, SEQ, D_MODEL), 1.0),
        "x_sample": nrm(ks[1], (DEC_BATCH, DEC_SEQ, D_MODEL), 1.0),
        "c_prompt": nrm(ks[2], (BATCH, D_MODEL), 1.0),
        "c_sample": nrm(ks[3], (DEC_BATCH, D_MODEL), 1.0),
        "w_ada": nrm(ks[4], (DEPTH, D_MODEL, N_MOD * D_MODEL), 0.5 * D_MODEL ** -0.5),
        "b_ada": nrm(ks[5], (DEPTH, N_MOD * D_MODEL), 0.01),
        "g_pre_mix": gain(ks[6], D_MODEL),
        "w_in": nrm(ks[7], (DEPTH, D_MODEL, in_cols), D_MODEL ** -0.5),
        "g_attn_out": gain(ks[8], ATTN_WIDTH),
        "w_fourier": nrm(ks[9], (DEPTH, N_FOURIER_GROUPS, FOURIER_GROUP_DIM, FOURIER_GROUP_DIM), FOURIER_GROUP_DIM ** -0.5),
        "g_fourier_out": gain(ks[10], FOURIER_WIDTH),
        "w_out": nrm(ks[11], (DEPTH, ATTN_WIDTH + FOURIER_WIDTH, D_MODEL), (ATTN_WIDTH + FOURIER_WIDTH) ** -0.5),
        "g_post_mix": gain(ks[12], D_MODEL),
        "g_pre_mlp": gain(ks[13], D_MODEL),
        "w_mlp_in": nrm(ks[14], (DEPTH, D_MODEL, D_FF), D_MODEL ** -0.5),
        "w_mlp_out": nrm(ks[15], (DEPTH, D_FF, D_MODEL), D_FF ** -0.5),
        "g_post_mlp": gain(ks[16], D_MODEL),
    }


def reference(x_prompt, x_sample, c_prompt, c_sample, w_ada, b_ada, g_pre_mix, w_in,
              g_attn_out, w_fourier, g_fourier_out, w_out, g_post_mix, g_pre_mlp,
              w_mlp_in, w_mlp_out, g_post_mlp):
    y_prompt = _trunk(x_prompt, c_prompt, w_ada, b_ada, g_pre_mix, w_in, g_attn_out,
                      w_fourier, g_fourier_out, w_out, g_post_mix, g_pre_mlp,
                      w_mlp_in, w_mlp_out, g_post_mlp)
    y_sample = _trunk(x_sample, c_sample, w_ada, b_ada, g_pre_mix, w_in, g_attn_out,
                      w_fourier, g_fourier_out, w_out, g_post_mix, g_pre_mlp,
                      w_mlp_in, w_mlp_out, g_post_mlp)
    return (y_prompt, y_sample)
```

```python
import functools
import math

import numpy as np
import jax
import jax.numpy as jnp
from jax import lax
from jax.experimental import pallas as pl
from jax.experimental.pallas import tpu as pltpu

D_MODEL = 2048
ATTN_WIDTH = 1024
FOURIER_WIDTH = 1024
HEAD_DIM = 128
N_HEADS = 8
N_GROUPS = 4
GROUP_DIM = 256
D_FF = 8192
IN_COLS = 3 * ATTN_WIDTH + FOURIER_WIDTH
DILATED_BRANCHES = ((128, 1), (512, 4), (2048, 16))
HALF = 64
N_MOD = 6
RMS_EPS = 1e-6
NEG_INF = -1e30

DFT_BASE = 2048
Q_SUB = 128
K_SUB = Q_SUB + 2 * HALF

F32 = jnp.float32
BF16 = jnp.bfloat16
MIB = 1024 * 1024

assert all(w // (2 * d) == HALF for w, d in DILATED_BRANCHES)


def _params(semantics, vmem_mib):
    return pltpu.CompilerParams(dimension_semantics=semantics, vmem_limit_bytes=vmem_mib * MIB)


def _rms_rows(v, gain):
    ms = jnp.mean(v * v, axis=-1, keepdims=True)
    return v * lax.rsqrt(ms + RMS_EPS) * gain


def _mod_kernel(c_ref, w_ref, b_ref, o_ref):
    c = c_ref[...]
    a = (c * jax.nn.sigmoid(c)).astype(BF16)
    o_ref[...] = jnp.dot(a, w_ref[...].astype(BF16), preferred_element_type=F32) + b_ref[...]


def _modulation(c_all, w_ada, b_ada):
    rows = c_all.shape[0]
    n = w_ada.shape[1]
    tn = 1024
    return pl.pallas_call(
        _mod_kernel,
        grid=(n // tn,),
        in_specs=[
            pl.BlockSpec((rows, D_MODEL), lambda j: (0, 0)),
            pl.BlockSpec((D_MODEL, tn), lambda j: (0, j)),
            pl.BlockSpec((1, tn), lambda j: (0, j)),
        ],
        out_specs=pl.BlockSpec((rows, tn), lambda j: (0, j)),
        out_shape=jax.ShapeDtypeStruct((rows, n), F32),
        compiler_params=_params(("parallel",), 40),
        name="mod",
    )(c_all, w_ada, b_ada.reshape(1, n))


def _in_proj_kernel(x_ref, mod_ref, g_ref, w_ref, z_ref, h_ref, *, tm, rc):
    j = pl.program_id(2)

    @pl.when(j == 0)
    def _():
        shift = mod_ref[0:1, :]
        scale1 = 1.0 + mod_ref[1:2, :]
        gain = g_ref[...]

        def body(c, carry):
            r0 = pl.multiple_of(c * rc, rc)
            h = _rms_rows(x_ref[pl.ds(r0, rc), :], gain) * scale1 + shift
            h_ref[pl.ds(r0, rc), :] = h.astype(BF16)
            return carry

        lax.fori_loop(0, tm // rc, body, 0)

    acc = jnp.dot(h_ref[...], w_ref[...], preferred_element_type=F32)
    qscale = jnp.where(j == 0, HEAD_DIM ** -0.5, 1.0).astype(F32)
    z_ref[...] = (acc * qscale).astype(BF16)


def _in_proj(x, mod, g_pre_mix, w_in_bf16):
    B, S, _ = x.shape
    tm, tn = 1024, 1024
    return pl.pallas_call(
        functools.partial(_in_proj_kernel, tm=tm, rc=128),
        grid=(B, S // tm, IN_COLS // tn),
        in_specs=[
            pl.BlockSpec((None, tm, D_MODEL), lambda b, i, j: (b, i, 0)),
            pl.BlockSpec((None, N_MOD, D_MODEL), lambda b, i, j: (b, 0, 0)),
            pl.BlockSpec((1, D_MODEL), lambda b, i, j: (0, 0)),
            pl.BlockSpec((D_MODEL, tn), lambda b, i, j: (0, j)),
        ],
        out_specs=pl.BlockSpec((None, tm, tn), lambda b, i, j: (b, i, j)),
        out_shape=jax.ShapeDtypeStruct((B, S, IN_COLS), BF16),
        scratch_shapes=[pltpu.VMEM((tm, D_MODEL), BF16)],
        compiler_params=_params(("parallel", "parallel", "arbitrary"), 48),
        name="in_proj",
    )(x, mod, g_pre_mix.reshape(1, D_MODEL), w_in_bf16)


def _attn_kernel(q_ref, kp_ref, kc_ref, kn_ref, vp_ref, vc_ref, vn_ref, bias_ref,
                 o_ref, lse_ref, kext, vext, *, lq, n_i):
    i = pl.program_id(2)
    kext[0:HALF, :] = kp_ref[...]
    kext[HALF:HALF + lq, :] = kc_ref[...]
    kext[HALF + lq:, :] = kn_ref[...]
    vext[0:HALF, :] = vp_ref[...]
    vext[HALF:HALF + lq, :] = vc_ref[...]
    vext[HALF + lq:, :] = vn_ref[...]

    n_sub = lq // Q_SUB
    col = lax.broadcasted_iota(jnp.int32, (Q_SUB, K_SUB), 1)
    lane = lax.broadcasted_iota(jnp.int32, (Q_SUB, 128), 1)
    for sb in range(n_sub):
        r0 = sb * Q_SUB
        lse_tile = jnp.zeros((Q_SUB, 128), F32)
        for h in range(N_HEADS):
            c0 = h * HEAD_DIM
            qh = q_ref[r0:r0 + Q_SUB, c0:c0 + HEAD_DIM]
            kh = kext[r0:r0 + K_SUB, c0:c0 + HEAD_DIM]
            vh = vext[r0:r0 + K_SUB, c0:c0 + HEAD_DIM]
            s = lax.dot_general(qh, kh, (((1,), (1,)), ((), ())), preferred_element_type=F32)
            s = s + bias_ref[h]
            if sb == 0:
                s = jnp.where((col >= HALF) | (i > 0), s, NEG_INF)
            if sb == n_sub - 1:
                s = jnp.where((col < Q_SUB + HALF) | (i < n_i - 1), s, NEG_INF)
            m = jnp.max(s, axis=-1, keepdims=True)
            p = jnp.exp(s - m)
            den = jnp.sum(p, axis=-1, keepdims=True)
            o = jnp.dot(p.astype(BF16), vh, preferred_element_type=F32) / den
            o_ref[r0:r0 + Q_SUB, c0:c0 + HEAD_DIM] = o.astype(BF16)
            lse_tile = jnp.where(lane == h, m + jnp.log(den), lse_tile)
        lse_ref[r0:r0 + Q_SUB, :] = lse_tile


def _branch_bias(dil):
    slopes = 2.0 ** (-8.0 * (np.arange(N_HEADS, dtype=np.float64) + 1.0) / N_HEADS)
    rel = np.abs((np.arange(K_SUB)[None, :] - HALF) - np.arange(Q_SUB)[:, None]).astype(np.float64)
    bias = -(slopes * dil)[:, None, None] * rel[None]
    bias = np.where(rel[None] <= HALF, bias, NEG_INF)
    return jnp.asarray(bias, dtype=F32)


def _attn_branch(z, dil):
    B, S, _ = z.shape
    L = S // dil
    lq = min(L, 512)
    n_i = L // lq
    hb = lq // HALF
    n_hb = L // HALF
    zv = z.reshape(B, L, dil * IN_COLS)
    aw = ATTN_WIDTH
    cols_per_pos = IN_COLS // aw

    def cur(part):
        return pl.BlockSpec((None, lq, aw), lambda b, r, i: (b, i, r * cols_per_pos + part))

    def prev(part):
        return pl.BlockSpec((None, HALF, aw),
                            lambda b, r, i: (b, jnp.maximum(i * hb - 1, 0), r * cols_per_pos + part))

    def nxt(part):
        return pl.BlockSpec((None, HALF, aw),
                            lambda b, r, i: (b, jnp.minimum((i + 1) * hb, n_hb - 1), r * cols_per_pos + part))

    o, lse = pl.pallas_call(
        functools.partial(_attn_kernel, lq=lq, n_i=n_i),
        grid=(B, dil, n_i),
        in_specs=[cur(0), prev(1), cur(1), nxt(1), prev(2), cur(2), nxt(2),
                  pl.BlockSpec((N_HEADS, Q_SUB, K_SUB), lambda b, r, i: (0, 0, 0))],
        out_specs=[
            pl.BlockSpec((None, lq, aw), lambda b, r, i: (b, i, r)),
            pl.BlockSpec((None, lq, 128), lambda b, r, i: (b, i, r)),
        ],
        out_shape=[
            jax.ShapeDtypeStruct((B, L, dil * aw), BF16),
            jax.ShapeDtypeStruct((B, L, dil * 128), F32),
        ],
        scratch_shapes=[pltpu.VMEM((lq + 2 * HALF, aw), BF16), pltpu.VMEM((lq + 2 * HALF, aw), BF16)],
        compiler_params=_params(("parallel", "parallel", "arbitrary"), 32),
        name=f"attn_d{dil}",
    )(zv, zv, zv, zv, zv, zv, zv, _branch_bias(dil))
    return o.reshape(B, S, aw), lse.reshape(B, S, 128)


def _dft_tables(n):
    n1 = 32
    n0 = n // n1
    k = np.arange(n, dtype=np.float64)[None, :]
    a = 2.0 * np.pi * ((np.arange(n0)[:, None] * n1 * k) % n) / n
    b = 2.0 * np.pi * ((np.arange(n1)[:, None] * k) % n) / n
    ca, sa = jnp.asarray(np.cos(a), F32), jnp.asarray(np.sin(a), F32)
    cb, sb = jnp.asarray(np.cos(b), F32), jnp.asarray(np.sin(b), F32)
    cos = (ca[:, None, :] * cb[None, :, :] - sa[:, None, :] * sb[None, :, :]).reshape(n, n)
    sin = (sa[:, None, :] * cb[None, :, :] + ca[:, None, :] * sb[None, :, :]).reshape(n, n)
    return cos.astype(BF16), (-sin).astype(BF16)


def _fold_kernel(c_ref, s_ref, w_ref, o_ref):
    w = w_ref[...]
    o_ref[:, :GROUP_DIM] = jnp.dot(c_ref[...], w, preferred_element_type=F32,
                                   precision=lax.Precision.HIGHEST).astype(BF16)
    o_ref[:, GROUP_DIM:] = jnp.dot(s_ref[...], w, preferred_element_type=F32,
                                   precision=lax.Precision.HIGHEST).astype(BF16)


def _fold_channel_dft(w_fourier):
    n = GROUP_DIM
    idx = (np.arange(n)[:, None] * np.arange(n)[None, :]) % n
    ang = 2.0 * np.pi * idx / n
    cos_c = jnp.asarray(np.cos(ang), F32)
    sin_c = jnp.asarray(np.sin(ang), F32)
    return pl.pallas_call(
        _fold_kernel,
        grid=(N_GROUPS,),
        in_specs=[
            pl.BlockSpec((n, n), lambda g: (0, 0)),
            pl.BlockSpec((n, n), lambda g: (0, 0)),
            pl.BlockSpec((None, n, n), lambda g: (g, 0, 0)),
        ],
        out_specs=pl.BlockSpec((None, n, 2 * n), lambda g: (g, 0, 0)),
        out_shape=jax.ShapeDtypeStruct((N_GROUPS, n, 2 * n), BF16),
        compiler_params=_params(("parallel",), 16),
        name="fold_channel_dft",
    )(cos_c, sin_c, w_fourier)


def _chan_kernel(u_ref, w_ref, va_ref, vb_ref):
    for g in range(N_GROUPS):
        c0 = g * GROUP_DIM
        v = jnp.dot(u_ref[:, c0:c0 + GROUP_DIM], w_ref[g], preferred_element_type=F32)
        va_ref[:, c0:c0 + GROUP_DIM] = v[:, :GROUP_DIM].astype(BF16)
        vb_ref[:, c0:c0 + GROUP_DIM] = v[:, GROUP_DIM:].astype(BF16)


def _channel_mix(z, wcs):
    B, S, _ = z.shape
    tm = 1024
    fw = FOURIER_WIDTH
    return pl.pallas_call(
        _chan_kernel,
        grid=(B, S // tm),
        in_specs=[
            pl.BlockSpec((None, tm, fw), lambda b, i: (b, i, IN_COLS // fw - 1)),
            pl.BlockSpec((N_GROUPS, GROUP_DIM, 2 * GROUP_DIM), lambda b, i: (0, 0, 0)),
        ],
        out_specs=[pl.BlockSpec((None, tm, fw), lambda b, i: (b, i, 0))] * 2,
        out_shape=[jax.ShapeDtypeStruct((B, S, fw), BF16)] * 2,
        compiler_params=_params(("parallel", "parallel"), 32),
        name="four_channel",
    )(z, wcs)


def _seq_dft_kernel(c_ref, ms_ref, va_ref, vb_ref, *out_refs, norm, both):
    c = c_ref[...]
    ms = ms_ref[...]
    va = va_ref[...]
    vb = vb_ref[...]
    y1 = jnp.dot(c, va, preferred_element_type=F32) + jnp.dot(ms, vb, preferred_element_type=F32)
    if both:
        y2 = jnp.dot(c, vb, preferred_element_type=F32) - jnp.dot(ms, va, preferred_element_type=F32)
        out_refs[0][...] = y1
        out_refs[1][...] = y2
    else:
        out_refs[0][...] = (y1 * norm).astype(BF16)


def _seq_dft(va, vb, cos_m, msin_m, radix, norm):
    B, S, fw = va.shape
    n = DFT_BASE
    assert S == radix * n
    tm = 512
    both = radix > 1
    vav = va.reshape(B, n, radix * fw)
    vbv = vb.reshape(B, n, radix * fw)
    v_spec = pl.BlockSpec((None, n, fw), lambda b, r, i: (b, 0, r))
    m_spec = pl.BlockSpec((tm, n), lambda b, r, i: (i, 0))
    o_spec = pl.BlockSpec((None, None, tm, fw), lambda b, r, i: (b, r, i, 0))
    if both:
        out_shape = [jax.ShapeDtypeStruct((B, radix, n, fw), F32)] * 2
        out_specs = [o_spec, o_spec]
    else:
        out_shape = [jax.ShapeDtypeStruct((B, radix, n, fw), BF16)]
        out_specs = [o_spec]
    outs = pl.pallas_call(
        functools.partial(_seq_dft_kernel, norm=norm, both=both),
        grid=(B, radix, n // tm),
        in_specs=[m_spec, m_spec, v_spec, v_spec],
        out_specs=out_specs,
        out_shape=out_shape,
        compiler_params=_params(("parallel", "parallel", "arbitrary"), 48),
        name=f"four_seq_r{radix}",
    )(cos_m, msin_m, vav, vbv)
    if both:
        return outs
    return outs[0].reshape(B, S, fw)


def _twiddle_kernel(y1_ref, y2_ref, tw_ref, f_ref, *, radix, norm):
    tw = tw_ref[...]
    acc = None
    for s in range(radix):
        term = tw[:, s:s + 1] * y1_ref[s] - tw[:, radix + s:radix + s + 1] * y2_ref[s]
        acc = term if acc is None else acc + term
    f_ref[...] = (acc * norm).astype(BF16)


def _twiddle_combine(y1, y2, radix, norm):
    B, _, n, fw = y1.shape
    S = radix * n
    tm = 512
    j = np.arange(S, dtype=np.float64)[:, None]
    ang = 2.0 * np.pi * ((j * np.arange(radix)[None, :]) % S) / S
    tw = np.zeros((S, 128), np.float32)
    tw[:, :radix] = np.cos(ang)
    tw[:, radix:2 * radix] = np.sin(ang)
    nb = n // tm
    return pl.pallas_call(
        functools.partial(_twiddle_kernel, radix=radix, norm=norm),
        grid=(B, radix, nb),
        in_specs=[
            pl.BlockSpec((None, radix, tm, fw), lambda b, m, i: (b, 0, i, 0)),
            pl.BlockSpec((None, radix, tm, fw), lambda b, m, i: (b, 0, i, 0)),
            pl.BlockSpec((tm, 128), lambda b, m, i: (m * nb + i, 0)),
        ],
        out_specs=pl.BlockSpec((None, tm, fw), lambda b, m, i: (b, m * nb + i, 0)),
        out_shape=jax.ShapeDtypeStruct((B, S, fw), BF16),
        compiler_params=_params(("parallel", "parallel", "arbitrary"), 48),
        name="four_twiddle",
    )(y1, y2, jnp.asarray(tw))


def _fourier(z, wcs, cos_m, msin_m):
    B, S, _ = z.shape
    radix = S // DFT_BASE
    norm = 1.0 / math.sqrt(S * GROUP_DIM)
    va, vb = _channel_mix(z, wcs)
    if radix == 1:
        return _seq_dft(va, vb, cos_m, msin_m, 1, norm)
    y1, y2 = _seq_dft(va, vb, cos_m, msin_m, radix, norm)
    return _twiddle_combine(y1, y2, radix, norm)


def _out_proj_kernel(o1_ref, o2_ref, o3_ref, l1_ref, l2_ref, l3_ref, f_ref, x_ref, mod_ref,
                     ga_ref, gf_ref, gpm_ref, gpl_ref, w_ref, x1_ref, h2_ref, mixed, ybuf, *, tm, rc):
    aw = ATTN_WIDTH

    def merge(c, carry):
        r0 = pl.multiple_of(c * rc, rc)
        rows = pl.ds(r0, rc)
        l1, l2, l3 = l1_ref[rows, :], l2_ref[rows, :], l3_ref[rows, :]
        top = jnp.maximum(jnp.maximum(l1, l2), l3)
        e1, e2, e3 = jnp.exp(l1 - top), jnp.exp(l2 - top), jnp.exp(l3 - top)
        inv = 1.0 / (e1 + e2 + e3)
        w1, w2, w3 = e1 * inv, e2 * inv, e3 * inv
        parts = []
        ssq = jnp.zeros((rc, 1), F32)
        for h in range(N_HEADS):
            cols = slice(h * HEAD_DIM, (h + 1) * HEAD_DIM)
            a = (w1[:, h:h + 1] * o1_ref[rows, cols].astype(F32)
                 + w2[:, h:h + 1] * o2_ref[rows, cols].astype(F32)
                 + w3[:, h:h + 1] * o3_ref[rows, cols].astype(F32))
            ssq = ssq + jnp.sum(a * a, axis=-1, keepdims=True)
            parts.append(a)
        inv_rms = lax.rsqrt(ssq * (1.0 / aw) + RMS_EPS)
        for h in range(N_HEADS):
            cols = slice(h * HEAD_DIM, (h + 1) * HEAD_DIM)
            mixed[rows, cols] = (parts[h] * inv_rms * ga_ref[:, cols]).astype(BF16)
        f = f_ref[rows, :].astype(F32)
        mixed[rows, aw:] = _rms_rows(f, gf_ref[...]).astype(BF16)
        return carry

    lax.fori_loop(0, tm // rc, merge, 0)

    ybuf[...] = jnp.dot(mixed[...], w_ref[...], preferred_element_type=F32)

    gate1 = mod_ref[2:3, :]
    shift2 = mod_ref[3:4, :]
    scale2 = 1.0 + mod_ref[4:5, :]

    def finish(c, carry):
        r0 = pl.multiple_of(c * rc, rc)
        rows = pl.ds(r0, rc)
        x1 = x_ref[rows, :] + gate1 * _rms_rows(ybuf[rows, :], gpm_ref[...])
        x1_ref[rows, :] = x1
        h2_ref[rows, :] = (_rms_rows(x1, gpl_ref[...]) * scale2 + shift2).astype(BF16)
        return carry

    lax.fori_loop(0, tm // rc, finish, 0)


def _out_proj(o_list, lse_list, f, x, mod, g_attn_out, g_fourier_out, g_post_mix, g_pre_mlp, w_out_bf16):
    B, S, _ = x.shape
    tm = 512
    aw, fw = ATTN_WIDTH, FOURIER_WIDTH
    row = lambda b, i: (b, i, 0)
    const2 = lambda b, i: (0, 0)
    return pl.pallas_call(
        functools.partial(_out_proj_kernel, tm=tm, rc=64),
        grid=(B, S // tm),
        in_specs=[pl.BlockSpec((None, tm, aw), row)] * 3
        + [pl.BlockSpec((None, tm, 128), row)] * 3
        + [
            pl.BlockSpec((None, tm, fw), row),
            pl.BlockSpec((None, tm, D_MODEL), row),
            pl.BlockSpec((None, N_MOD, D_MODEL), lambda b, i: (b, 0, 0)),
            pl.BlockSpec((1, aw), const2),
            pl.BlockSpec((1, fw), const2),
            pl.BlockSpec((1, D_MODEL), const2),
            pl.BlockSpec((1, D_MODEL), const2),
            pl.BlockSpec((aw + fw, D_MODEL), const2),
        ],
        out_specs=[pl.BlockSpec((None, tm, D_MODEL), row)] * 2,
        out_shape=[jax.ShapeDtypeStruct((B, S, D_MODEL), F32), jax.ShapeDtypeStruct((B, S, D_MODEL), BF16)],
        scratch_shapes=[pltpu.VMEM((tm, aw + fw), BF16), pltpu.VMEM((tm, D_MODEL), F32)],
        compiler_params=_params(("parallel", "parallel"), 56),
        name="out_proj",
    )(*o_list, *lse_list, f, x, mod, g_attn_out.reshape(1, aw), g_fourier_out.reshape(1, fw),
      g_post_mix.reshape(1, D_MODEL), g_pre_mlp.reshape(1, D_MODEL), w_out_bf16)


def _mlp_kernel(h_ref, x1_ref, mod_ref, g_ref, w1_ref, w2_ref, out_ref, *, tm, rc, n_j):
    j = pl.program_id(2)
    hid = jnp.dot(h_ref[...], w1_ref[...], preferred_element_type=F32)
    hid = jnp.square(jnp.maximum(hid, 0.0)).astype(BF16)
    part = jnp.dot(hid, w2_ref[...], preferred_element_type=F32)

    @pl.when(j == 0)
    def _():
        out_ref[...] = part

    @pl.when(j > 0)
    def _():
        out_ref[...] += part

    @pl.when(j == n_j - 1)
    def _():
        gate2 = mod_ref[5:6, :]

        def body(c, carry):
            r0 = pl.multiple_of(c * rc, rc)
            rows = pl.ds(r0, rc)
            out_ref[rows, :] = x1_ref[rows, :] + gate2 * _rms_rows(out_ref[rows, :], g_ref[...])
            return carry

        lax.fori_loop(0, tm // rc, body, 0)


def _mlp(h2, x1, mod, g_post_mlp, w1_bf16, w2_bf16):
    B, S, _ = x1.shape
    tm, tf = 512, 512
    n_j = D_FF // tf
    row = lambda b, i, j: (b, i, 0)
    return pl.pallas_call(
        functools.partial(_mlp_kernel, tm=tm, rc=128, n_j=n_j),
        grid=(B, S // tm, n_j),
        in_specs=[
            pl.BlockSpec((None, tm, D_MODEL), row),
            pl.BlockSpec((None, tm, D_MODEL), row),
            pl.BlockSpec((None, N_MOD, D_MODEL), lambda b, i, j: (b, 0, 0)),
            pl.BlockSpec((1, D_MODEL), lambda b, i, j: (0, 0)),
            pl.BlockSpec((D_MODEL, tf), lambda b, i, j: (0, j)),
            pl.BlockSpec((tf, D_MODEL), lambda b, i, j: (j, 0)),
        ],
        out_specs=pl.BlockSpec((None, tm, D_MODEL), row),
        out_shape=jax.ShapeDtypeStruct((B, S, D_MODEL), F32),
        compiler_params=_params(("parallel", "parallel", "arbitrary"), 48),
        name="mlp",
    )(h2, x1, mod, g_post_mlp.reshape(1, D_MODEL), w1_bf16, w2_bf16)


def _layer(x, mod, p):
    z = _in_proj(x, mod, p["g_pre_mix"], p["w_in"])
    outs = [_attn_branch(z, dil) for _, dil in DILATED_BRANCHES]
    f = _fourier(z, p["wcs"], p["cos_m"], p["msin_m"])
    x1, h2 = _out_proj([o for o, _ in outs], [l for _, l in outs], f, x, mod,
                       p["g_attn_out"], p["g_fourier_out"], p["g_post_mix"], p["g_pre_mlp"], p["w_out"])
    return _mlp(h2, x1, mod, p["g_post_mlp"], p["w_mlp_in"], p["w_mlp_out"])


def kernel(x_prompt, x_sample, c_prompt, c_sample, w_ada, b_ada, g_pre_mix, w_in, g_attn_out, w_fourier,
           g_fourier_out, w_out, g_post_mix, g_pre_mlp, w_mlp_in, w_mlp_out, g_post_mlp):
    depth = w_ada.shape[0]
    nb_p, nb_s = c_prompt.shape[0], c_sample.shape[0]
    pad = (-(nb_p + nb_s)) % 8
    c_all = jnp.concatenate([c_prompt, c_sample, jnp.zeros((pad, D_MODEL), F32)], axis=0)
    cos_m, msin_m = _dft_tables(DFT_BASE)
    xp, xs = x_prompt, x_sample
    for l in range(depth):
        mod = _modulation(c_all, w_ada[l], b_ada[l])
        mod_p = mod[:nb_p].reshape(nb_p, N_MOD, D_MODEL)
        mod_s = mod[nb_p:nb_p + nb_s].reshape(nb_s, N_MOD, D_MODEL)
        p = dict(
            g_pre_mix=g_pre_mix[l], w_in=w_in[l].astype(BF16), g_attn_out=g_attn_out[l],
            g_fourier_out=g_fourier_out[l], w_out=w_out[l].astype(BF16), g_post_mix=g_post_mix[l],
            g_pre_mlp=g_pre_mlp[l], w_mlp_in=w_mlp_in[l].astype(BF16), w_mlp_out=w_mlp_out[l].astype(BF16),
            g_post_mlp=g_post_mlp[l], wcs=_fold_channel_dft(w_fourier[l]), cos_m=cos_m, msin_m=msin_m,
        )
        xp = _layer(xp, mod_p, p)
        xs = _layer(xs, mod_s, p)
    return (xp, xs)
```

```python
import functools
import math

import numpy as np
import jax
import jax.numpy as jnp
from jax import lax
from jax.experimental import pallas as pl
from jax.experimental.pallas import tpu as pltpu

D_MODEL = 2048
ATTN_WIDTH = 1024
FOURIER_WIDTH = 1024
HEAD_DIM = 128
N_HEADS = 8
N_GROUPS = 4
GROUP_DIM = 256
D_FF = 8192
IN_COLS = 3 * ATTN_WIDTH + FOURIER_WIDTH
DILATED_BRANCHES = ((128, 1), (512, 4), (2048, 16))
HALF = 64
N_MOD = 6
RMS_EPS = 1e-6
NEG_INF = -1e30

SEG = 2048
N_PLANES = 4
PLANE_ROWS = SEG // N_PLANES
N_CLASSES = 16
CLASS_ROWS = SEG // N_CLASSES
LANES = 128
D_SLABS = D_MODEL // LANES
Q_SUB = 128
K_SUB = Q_SUB + 2 * HALF

F32 = jnp.float32
BF16 = jnp.bfloat16
MIB = 1024 * 1024

assert DILATED_BRANCHES == ((128, 1), (512, 4), (2048, 16))
assert all(w // (2 * d) == HALF for w, d in DILATED_BRANCHES)


def _params(semantics, vmem_mib):
    return pltpu.CompilerParams(dimension_semantics=semantics, vmem_limit_bytes=vmem_mib * MIB)


def _rms_rows(v, gain):
    ms = jnp.mean(v * v, axis=-1, keepdims=True)
    return v * lax.rsqrt(ms + RMS_EPS) * gain


def _lane_slab(s):
    return slice(s * LANES, (s + 1) * LANES)


def _mod_kernel(c_ref, w_ref, b_ref, o_ref):
    c = c_ref[...]
    a = (c * jax.nn.sigmoid(c)).astype(BF16)
    o_ref[...] = jnp.dot(a, w_ref[...].astype(BF16), preferred_element_type=F32) + b_ref[...]


def _modulation(c_all, w_ada, b_ada):
    rows = c_all.shape[0]
    n = w_ada.shape[1]
    tn = 1024
    return pl.pallas_call(
        _mod_kernel,
        grid=(n // tn,),
        in_specs=[
            pl.BlockSpec((rows, D_MODEL), lambda j: (0, 0)),
            pl.BlockSpec((D_MODEL, tn), lambda j: (0, j)),
            pl.BlockSpec((1, tn), lambda j: (0, j)),
        ],
        out_specs=pl.BlockSpec((rows, tn), lambda j: (0, j)),
        out_shape=jax.ShapeDtypeStruct((rows, n), F32),
        compiler_params=_params(("parallel",), 40),
        name="mod",
    )(c_all, w_ada, b_ada.reshape(1, n))


def _in_proj_kernel(x_ref, mod_ref, g_ref, w_ref, z4_ref, z16_ref, h_ref, hslab, aslab, *, tm, tn, nb):
    j = pl.program_id(2)
    pr = tm // N_PLANES
    pb = nb // N_PLANES

    @pl.when(j == 0)
    def _():
        shift = mod_ref[0:1, :]
        scale1 = 1.0 + mod_ref[1:2, :]
        gain = g_ref[...]

        def block(c, carry):
            n0 = pl.multiple_of(c * nb, nb)
            for q in range(nb // 64):
                h = _rms_rows(x_ref[pl.ds(n0 + q * 64, 64), :], gain) * scale1 + shift
                for s in range(D_SLABS):
                    hslab[s, q * 64:(q + 1) * 64, :] = h[:, _lane_slab(s)]
            m0 = pl.multiple_of(c * pb, pb)
            for r in range(N_PLANES):
                for s in range(D_SLABS):
                    h_ref[pl.ds(r * pr + m0, pb), _lane_slab(s)] = (
                        hslab[s, pl.ds(r, pb, stride=N_PLANES), :].astype(BF16))
            return carry

        lax.fori_loop(0, tm // nb, block, 0)

    acc = jnp.dot(h_ref[...], w_ref[...], preferred_element_type=F32)
    acc = acc * jnp.where(j == 1, HEAD_DIM ** -0.5, 1.0).astype(F32)
    for r in range(N_PLANES):
        z4_ref[r] = acc[r * pr:(r + 1) * pr].astype(BF16)
    for s in range(tn // LANES):
        aslab[s] = acc[:, _lane_slab(s)]
    cr = pr // 4
    for r in range(N_PLANES):
        for a in range(4):
            for s in range(tn // LANES):
                z16_ref[r + 4 * a, :, _lane_slab(s)] = (
                    aslab[s, pl.ds(r * pr + a, cr, stride=4), :].astype(BF16))


def _in_proj(xg, mod, nseg, g_pre_mix, w_in_perm):
    G = xg.shape[0]
    tm, tn = 1024, 1024
    n_j = IN_COLS // tn
    return pl.pallas_call(
        functools.partial(_in_proj_kernel, tm=tm, tn=tn, nb=256),
        grid=(G, SEG // tm, n_j),
        in_specs=[
            pl.BlockSpec((None, tm, D_MODEL), lambda g, i, j: (g, i, 0)),
            pl.BlockSpec((None, N_MOD, D_MODEL), lambda g, i, j: (g // nseg, 0, 0)),
            pl.BlockSpec((1, D_MODEL), lambda g, i, j: (0, 0)),
            pl.BlockSpec((D_MODEL, tn), lambda g, i, j: (0, j)),
        ],
        out_specs=[
            pl.BlockSpec((None, N_PLANES, tm // N_PLANES, tn), lambda g, i, j: (g, 0, i, j)),
            pl.BlockSpec((None, N_CLASSES, tm // N_CLASSES, tn), lambda g, i, j: (g, 0, i, jnp.maximum(j, 1) - 1)),
        ],
        out_shape=[
            jax.ShapeDtypeStruct((G, N_PLANES, PLANE_ROWS, IN_COLS), BF16),
            jax.ShapeDtypeStruct((G, N_CLASSES, CLASS_ROWS, 3 * ATTN_WIDTH), BF16),
        ],
        scratch_shapes=[
            pltpu.VMEM((tm, D_MODEL), BF16),
            pltpu.VMEM((D_SLABS, 256, LANES), F32),
            pltpu.VMEM((tn // LANES, tm, LANES), F32),
        ],
        compiler_params=_params(("parallel", "parallel", "arbitrary"), 56),
        name="in_proj",
    )(xg, mod, g_pre_mix.reshape(1, D_MODEL), w_in_perm)


def _softmax_pv(s, vh):
    m = jnp.max(s, axis=-1, keepdims=True)
    p = jnp.exp(s - m)
    den = jnp.sum(p, axis=-1, keepdims=True)
    o = jnp.dot(p.astype(BF16), vh, preferred_element_type=F32) / den
    return o, m + jnp.log(den)


def _scores(qh, kh, bias):
    return lax.dot_general(qh, kh, (((1,), (1,)), ((), ())), preferred_element_type=F32) + bias


def _band_bias(dil):
    slopes = 2.0 ** (-8.0 * (np.arange(N_HEADS, dtype=np.float64) + 1.0) / N_HEADS)
    rel = np.abs((np.arange(K_SUB)[None, :] - HALF) - np.arange(Q_SUB)[:, None]).astype(np.float64)
    bias = -(slopes * dil)[:, None, None] * rel[None]
    return jnp.asarray(np.where(rel[None] <= HALF, bias, NEG_INF), dtype=F32)


def _plane_bias():
    slopes = 2.0 ** (-8.0 * (np.arange(N_HEADS, dtype=np.float64) + 1.0) / N_HEADS)
    qr, qm = np.divmod(np.arange(Q_SUB), Q_SUB // N_PLANES)
    kr, km = np.divmod(np.arange(K_SUB), K_SUB // N_PLANES)
    rel = np.abs(N_PLANES * (km[None, :] - HALF // N_PLANES - qm[:, None]) + (kr[None, :] - qr[:, None])).astype(np.float64)
    bias = -slopes[:, None, None] * rel[None]
    return jnp.asarray(np.where(rel[None] <= HALF, bias, NEG_INF), dtype=F32)


def _attn_class_kernel(q_ref, kp_ref, kc_ref, kn_ref, vp_ref, vc_ref, vn_ref, bias_ref,
                       o_ref, lse_ref, kext, vext, *, lq, n_i):
    i = pl.program_id(2)
    kext[0:HALF, :] = kp_ref[...]
    kext[HALF:HALF + lq, :] = kc_ref[...]
    kext[HALF + lq:, :] = kn_ref[...]
    vext[0:HALF, :] = vp_ref[...]
    vext[HALF:HALF + lq, :] = vc_ref[...]
    vext[HALF + lq:, :] = vn_ref[...]

    n_sub = lq // Q_SUB
    col = lax.broadcasted_iota(jnp.int32, (Q_SUB, K_SUB), 1)
    lane = lax.broadcasted_iota(jnp.int32, (Q_SUB, LANES), 1)
    for sb in range(n_sub):
        r0 = sb * Q_SUB
        lse_tile = jnp.zeros((Q_SUB, LANES), F32)
        for h in range(N_HEADS):
            hc = _lane_slab(h)
            s = _scores(q_ref[r0:r0 + Q_SUB, hc], kext[r0:r0 + K_SUB, hc], bias_ref[h])
            if sb == 0:
                s = jnp.where((col >= HALF) | (i > 0), s, NEG_INF)
            if sb == n_sub - 1:
                s = jnp.where((col < Q_SUB + HALF) | (i < n_i - 1), s, NEG_INF)
            o, lse = _softmax_pv(s, vext[r0:r0 + K_SUB, hc])
            o_ref[r0:r0 + Q_SUB, hc] = o.astype(BF16)
            lse_tile = jnp.where(lane == h, lse, lse_tile)
        lse_ref[r0:r0 + Q_SUB, :] = lse_tile


def _attn_d4(z4, B, nseg):
    aw = ATTN_WIDTH
    lq = PLANE_ROWS
    hb = lq // HALF

    def cur(part):
        return pl.BlockSpec((None, None, None, lq, aw), lambda b, r, i: (b, i, r, 0, part))

    def prev(part):
        return pl.BlockSpec((None, None, None, HALF, aw), lambda b, r, i: (b, jnp.maximum(i - 1, 0), r, hb - 1, part))

    def nxt(part):
        return pl.BlockSpec((None, None, None, HALF, aw), lambda b, r, i: (b, jnp.minimum(i + 1, nseg - 1), r, 0, part))

    return pl.pallas_call(
        functools.partial(_attn_class_kernel, lq=lq, n_i=nseg),
        grid=(B, N_PLANES, nseg),
        in_specs=[cur(1), prev(2), cur(2), nxt(2), prev(3), cur(3), nxt(3),
                  pl.BlockSpec((N_HEADS, Q_SUB, K_SUB), lambda b, r, i: (0, 0, 0))],
        out_specs=[
            pl.BlockSpec((None, None, None, lq, aw), lambda b, r, i: (b, i, r, 0, 0)),
            pl.BlockSpec((None, None, None, lq, LANES), lambda b, r, i: (b, i, r, 0, 0)),
        ],
        out_shape=[
            jax.ShapeDtypeStruct((B, nseg, N_PLANES, PLANE_ROWS, aw), BF16),
            jax.ShapeDtypeStruct((B, nseg, N_PLANES, PLANE_ROWS, LANES), F32),
        ],
        scratch_shapes=[pltpu.VMEM((lq + 2 * HALF, aw), BF16), pltpu.VMEM((lq + 2 * HALF, aw), BF16)],
        compiler_params=_params(("parallel", "parallel", "arbitrary"), 32),
        name="attn_d4",
    )(z4, z4, z4, z4, z4, z4, z4, _band_bias(4))


def _attn_d16_kernel(q_ref, kp_ref, kc_ref, kn_ref, vp_ref, vc_ref, vn_ref, bias_ref, o_ref, lse_ref, *, n_i):
    i = pl.program_id(2)
    col = lax.broadcasted_iota(jnp.int32, (Q_SUB, K_SUB), 1)
    lane = lax.broadcasted_iota(jnp.int32, (Q_SUB, LANES), 1)
    for a in range(4):
        lse_tile = jnp.zeros((Q_SUB, LANES), F32)
        for h in range(N_HEADS):
            hc = _lane_slab(h)
            kh = jnp.concatenate([kp_ref[a, :, hc], kc_ref[a, :, hc], kn_ref[a, :, hc]], axis=0)
            vh = jnp.concatenate([vp_ref[a, :, hc], vc_ref[a, :, hc], vn_ref[a, :, hc]], axis=0)
            s = _scores(q_ref[a, :, hc], kh, bias_ref[h])
            s = jnp.where((col >= HALF) | (i > 0), s, NEG_INF)
            s = jnp.where((col < Q_SUB + HALF) | (i < n_i - 1), s, NEG_INF)
            o, lse = _softmax_pv(s, vh)
            o_ref[h, pl.ds(a, CLASS_ROWS, stride=4), :] = o
            lse_tile = jnp.where(lane == h, lse, lse_tile)
        lse_ref[pl.ds(a, CLASS_ROWS, stride=4), :] = lse_tile


def _attn_d16(z16, B, nseg):
    aw = ATTN_WIDTH
    lq = CLASS_ROWS
    assert lq == Q_SUB
    hb = lq // HALF

    def cur(part):
        return pl.BlockSpec((None, None, 4, None, lq, aw), lambda b, r, i: (b, i, 0, r, 0, part))

    def prev(part):
        return pl.BlockSpec((None, None, 4, None, HALF, aw),
                            lambda b, r, i: (b, jnp.maximum(i - 1, 0), 0, r, hb - 1, part))

    def nxt(part):
        return pl.BlockSpec((None, None, 4, None, HALF, aw),
                            lambda b, r, i: (b, jnp.minimum(i + 1, nseg - 1), 0, r, 0, part))

    return pl.pallas_call(
        functools.partial(_attn_d16_kernel, n_i=nseg),
        grid=(B, N_PLANES, nseg),
        in_specs=[cur(0), prev(1), cur(1), nxt(1), prev(2), cur(2), nxt(2),
                  pl.BlockSpec((N_HEADS, Q_SUB, K_SUB), lambda b, r, i: (0, 0, 0))],
        out_specs=[
            pl.BlockSpec((None, None, None, N_HEADS, PLANE_ROWS, LANES), lambda b, r, i: (b, i, r, 0, 0, 0)),
            pl.BlockSpec((None, None, None, PLANE_ROWS, LANES), lambda b, r, i: (b, i, r, 0, 0)),
        ],
        out_shape=[
            jax.ShapeDtypeStruct((B, nseg, N_PLANES, N_HEADS, PLANE_ROWS, LANES), F32),
            jax.ShapeDtypeStruct((B, nseg, N_PLANES, PLANE_ROWS, LANES), F32),
        ],
        compiler_params=_params(("parallel", "parallel", "arbitrary"), 32),
        name="attn_d16",
    )(z16, z16, z16, z16, z16, z16, z16, _band_bias(16))


def _attn_d1_kernel(q_ref, kp_ref, kc_ref, kn_ref, vp_ref, vc_ref, vn_ref, bias_ref,
                    o_ref, lse_ref, kext, vext, *, mq, n_t):
    t = pl.program_id(1)
    qm = Q_SUB // N_PLANES
    km = K_SUB // N_PLANES
    hm = HALF // N_PLANES
    for r in range(N_PLANES):
        kext[r, 0:hm, :] = kp_ref[r]
        kext[r, hm:hm + mq, :] = kc_ref[r]
        kext[r, hm + mq:, :] = kn_ref[r]
        vext[r, 0:hm, :] = vp_ref[r]
        vext[r, hm:hm + mq, :] = vc_ref[r]
        vext[r, hm + mq:, :] = vn_ref[r]

    n_sub = mq // qm
    colm = lax.broadcasted_iota(jnp.int32, (Q_SUB, K_SUB), 1) & (km - 1)
    lane = lax.broadcasted_iota(jnp.int32, (Q_SUB, LANES), 1)
    for sb in range(n_sub):
        m1 = sb * qm
        lse_tile = jnp.zeros((Q_SUB, LANES), F32)
        for h in range(N_HEADS):
            hc = _lane_slab(h)
            qh = jnp.concatenate([q_ref[r, m1:m1 + qm, hc] for r in range(N_PLANES)], axis=0)
            kh = jnp.concatenate([kext[r, m1:m1 + km, hc] for r in range(N_PLANES)], axis=0)
            vh = jnp.concatenate([vext[r, m1:m1 + km, hc] for r in range(N_PLANES)], axis=0)
            s = _scores(qh, kh, bias_ref[h])
            if sb == 0:
                s = jnp.where((colm >= hm) | (t > 0), s, NEG_INF)
            if sb == n_sub - 1:
                s = jnp.where((colm < km - hm) | (t < n_t - 1), s, NEG_INF)
            o, lse = _softmax_pv(s, vh)
            for r in range(N_PLANES):
                o_ref[r, m1:m1 + qm, hc] = o[r * qm:(r + 1) * qm].astype(BF16)
            lse_tile = jnp.where(lane == h, lse, lse_tile)
        for r in range(N_PLANES):
            lse_ref[r, m1:m1 + qm, :] = lse_tile[r * qm:(r + 1) * qm]


def _attn_d1(z4, B, nseg):
    aw = ATTN_WIDTH
    mq = 128
    hm = HALF // N_PLANES
    per_seg = PLANE_ROWS // mq
    hb_step = mq // hm
    hb_seg = PLANE_ROWS // hm
    n_t = nseg * per_seg

    def cur(part):
        return pl.BlockSpec((None, None, N_PLANES, mq, aw), lambda b, t: (b, t // per_seg, 0, t % per_seg, part))

    def prev(part):
        def imap(b, t):
            q = jnp.maximum(t * hb_step - 1, 0)
            return (b, q // hb_seg, 0, q % hb_seg, part)
        return pl.BlockSpec((None, None, N_PLANES, hm, aw), imap)

    def nxt(part):
        def imap(b, t):
            q = jnp.minimum((t + 1) * hb_step, nseg * hb_seg - 1)
            return (b, q // hb_seg, 0, q % hb_seg, part)
        return pl.BlockSpec((None, None, N_PLANES, hm, aw), imap)

    return pl.pallas_call(
        functools.partial(_attn_d1_kernel, mq=mq, n_t=n_t),
        grid=(B, n_t),
        in_specs=[cur(1), prev(2), cur(2), nxt(2), prev(3), cur(3), nxt(3),
                  pl.BlockSpec((N_HEADS, Q_SUB, K_SUB), lambda b, t: (0, 0, 0))],
        out_specs=[
            pl.BlockSpec((None, None, N_PLANES, mq, aw), lambda b, t: (b, t // per_seg, 0, t % per_seg, 0)),
            pl.BlockSpec((None, None, N_PLANES, mq, LANES), lambda b, t: (b, t // per_seg, 0, t % per_seg, 0)),
        ],
        out_shape=[
            jax.ShapeDtypeStruct((B, nseg, N_PLANES, PLANE_ROWS, aw), BF16),
            jax.ShapeDtypeStruct((B, nseg, N_PLANES, PLANE_ROWS, LANES), F32),
        ],
        scratch_shapes=[pltpu.VMEM((N_PLANES, mq + 2 * hm, aw), BF16), pltpu.VMEM((N_PLANES, mq + 2 * hm, aw), BF16)],
        compiler_params=_params(("parallel", "arbitrary"), 32),
        name="attn_d1",
    )(z4, z4, z4, z4, z4, z4, z4, _plane_bias())


def _plane_rows(m):
    rest = m.shape[1:]
    return m.reshape((PLANE_ROWS, N_PLANES) + rest).swapaxes(0, 1).reshape((SEG,) + rest)


def _dft_tables():
    n = SEG
    n1 = 32
    n0 = n // n1
    k = np.arange(n, dtype=np.float64)[None, :]
    a = 2.0 * np.pi * ((np.arange(n0)[:, None] * n1 * k) % n) / n
    b = 2.0 * np.pi * ((np.arange(n1)[:, None] * k) % n) / n
    ca, sa = jnp.asarray(np.cos(a), F32), jnp.asarray(np.sin(a), F32)
    cb, sb = jnp.asarray(np.cos(b), F32), jnp.asarray(np.sin(b), F32)
    cos = (ca[:, None, :] * cb[None, :, :] - sa[:, None, :] * sb[None, :, :]).reshape(n, n)
    sin = (sa[:, None, :] * cb[None, :, :] + ca[:, None, :] * sb[None, :, :]).reshape(n, n)
    cos_r = _plane_rows(cos).astype(BF16)
    msin_r = _plane_rows(-sin).astype(BF16)
    cos_rc = _plane_rows(cos_r.T).T
    msin_rc = _plane_rows(msin_r.T).T
    return (cos_r, msin_r), (cos_rc, msin_rc)


def _fold_kernel(c_ref, s_ref, w_ref, o_ref):
    w = w_ref[...]
    o_ref[:, :GROUP_DIM] = jnp.dot(c_ref[...], w, preferred_element_type=F32,
                                   precision=lax.Precision.HIGHEST).astype(BF16)
    o_ref[:, GROUP_DIM:] = jnp.dot(s_ref[...], w, preferred_element_type=F32,
                                   precision=lax.Precision.HIGHEST).astype(BF16)


def _fold_channel_dft(w_fourier):
    n = GROUP_DIM
    idx = (np.arange(n)[:, None] * np.arange(n)[None, :]) % n
    ang = 2.0 * np.pi * idx / n
    cos_c = jnp.asarray(np.cos(ang), F32)
    sin_c = jnp.asarray(np.sin(ang), F32)
    return pl.pallas_call(
        _fold_kernel,
        grid=(N_GROUPS,),
        in_specs=[
            pl.BlockSpec((n, n), lambda g: (0, 0)),
            pl.BlockSpec((n, n), lambda g: (0, 0)),
            pl.BlockSpec((None, n, n), lambda g: (g, 0, 0)),
        ],
        out_specs=pl.BlockSpec((None, n, 2 * n), lambda g: (g, 0, 0)),
        out_shape=jax.ShapeDtypeStruct((N_GROUPS, n, 2 * n), BF16),
        compiler_params=_params(("parallel",), 16),
        name="fold_channel_dft",
    )(cos_c, sin_c, w_fourier)


def _chan_kernel(u_ref, w_ref, va_ref, vb_ref):
    for g in range(N_GROUPS):
        c0 = g * GROUP_DIM
        v = jnp.dot(u_ref[:, c0:c0 + GROUP_DIM], w_ref[g], preferred_element_type=F32)
        va_ref[:, c0:c0 + GROUP_DIM] = v[:, :GROUP_DIM].astype(BF16)
        vb_ref[:, c0:c0 + GROUP_DIM] = v[:, GROUP_DIM:].astype(BF16)


def _channel_mix(z4, wcs):
    G = z4.shape[0]
    fw = FOURIER_WIDTH
    spec = pl.BlockSpec((None, None, PLANE_ROWS, fw), lambda g, r: (g, r, 0, 0))
    return pl.pallas_call(
        _chan_kernel,
        grid=(G, N_PLANES),
        in_specs=[spec, pl.BlockSpec((N_GROUPS, GROUP_DIM, 2 * GROUP_DIM), lambda g, r: (0, 0, 0))],
        out_specs=[spec, spec],
        out_shape=[jax.ShapeDtypeStruct((G, N_PLANES, PLANE_ROWS, fw), BF16)] * 2,
        compiler_params=_params(("parallel", "parallel"), 32),
        name="four_channel",
    )(z4, wcs)


def _seq_dft_kernel(c_ref, ms_ref, va_ref, vb_ref, *out_refs, norm, both):
    c = c_ref[...]
    ms = ms_ref[...]
    va = va_ref[...].reshape(SEG, FOURIER_WIDTH)
    vb = vb_ref[...].reshape(SEG, FOURIER_WIDTH)
    y1 = jnp.dot(c, va, preferred_element_type=F32) + jnp.dot(ms, vb, preferred_element_type=F32)
    if both:
        y2 = jnp.dot(c, vb, preferred_element_type=F32) - jnp.dot(ms, va, preferred_element_type=F32)
        out_refs[0][...] = y1
        out_refs[1][...] = y2
    else:
        out_refs[0][...] = (y1 * norm).astype(BF16)


def _seq_dft(va, vb, tables, B, nseg, norm):
    fw = FOURIER_WIDTH
    tm = 512
    both = nseg > 1
    if both:
        assert nseg == N_PLANES
        n_cls = N_PLANES
        v_spec = pl.BlockSpec((None, nseg, None, PLANE_ROWS, fw), lambda b, s, i: (b, 0, s, 0, 0))
    else:
        n_cls = 1
        v_spec = pl.BlockSpec((None, None, N_PLANES, PLANE_ROWS, fw), lambda b, s, i: (b, 0, 0, 0, 0))
    m_spec = pl.BlockSpec((tm, SEG), lambda b, s, i: (i, 0))
    o_spec = pl.BlockSpec((None, None, tm, fw), lambda b, s, i: (b, s, i, 0))
    odt = F32 if both else BF16
    outs = pl.pallas_call(
        functools.partial(_seq_dft_kernel, norm=norm, both=both),
        grid=(B, n_cls, SEG // tm),
        in_specs=[m_spec, m_spec, v_spec, v_spec],
        out_specs=[o_spec] * (2 if both else 1),
        out_shape=[jax.ShapeDtypeStruct((B, n_cls, SEG, fw), odt)] * (2 if both else 1),
        compiler_params=_params(("parallel", "parallel", "arbitrary"), 48),
        name="four_seq_r4" if both else "four_seq_r1",
    )(tables[0], tables[1], va, vb)
    return outs if both else outs[0]


def _twiddle_kernel(y1_ref, y2_ref, tw_ref, f_ref, *, radix, norm):
    tw = tw_ref[...]
    acc = None
    for s in range(radix):
        term = tw[:, s:s + 1] * y1_ref[s] - tw[:, radix + s:radix + s + 1] * y2_ref[s]
        acc = term if acc is None else acc + term
    f_ref[...] = (acc * norm).astype(BF16)


def _twiddle_combine(y1, y2, norm):
    B, radix, _, fw = y1.shape
    nseg = radix
    S = nseg * SEG
    tm = 256
    g, r, m = np.meshgrid(np.arange(nseg), np.arange(N_PLANES), np.arange(PLANE_ROWS), indexing="ij")
    j = (SEG * g + N_PLANES * m + r).astype(np.float64)[..., None]
    ang = 2.0 * np.pi * ((j * np.arange(radix)) % S) / S
    tw = np.zeros((nseg, N_PLANES, PLANE_ROWS, LANES), np.float32)
    tw[..., :radix] = np.cos(ang)
    tw[..., radix:2 * radix] = np.sin(ang)
    nb = PLANE_ROWS // tm
    y_spec = pl.BlockSpec((None, radix, tm, fw), lambda b, g, r, i: (b, 0, r * nb + i, 0))
    return pl.pallas_call(
        functools.partial(_twiddle_kernel, radix=radix, norm=norm),
        grid=(B, nseg, N_PLANES, nb),
        in_specs=[y_spec, y_spec, pl.BlockSpec((None, None, tm, LANES), lambda b, g, r, i: (g, r, i, 0))],
        out_specs=pl.BlockSpec((None, None, None, tm, fw), lambda b, g, r, i: (b, g, r, i, 0)),
        out_shape=jax.ShapeDtypeStruct((B, nseg, N_PLANES, PLANE_ROWS, fw), BF16),
        compiler_params=_params(("parallel", "parallel", "parallel", "arbitrary"), 48),
        name="four_twiddle",
    )(y1, y2, jnp.asarray(tw))


def _fourier(z4, wcs, tables_nat, tables_plane, B, nseg):
    G = B * nseg
    fw = FOURIER_WIDTH
    norm = 1.0 / math.sqrt(nseg * SEG * GROUP_DIM)
    va, vb = _channel_mix(z4, wcs)
    va = va.reshape(B, nseg, N_PLANES, PLANE_ROWS, fw)
    vb = vb.reshape(B, nseg, N_PLANES, PLANE_ROWS, fw)
    if nseg == 1:
        f = _seq_dft(va, vb, tables_plane, B, nseg, norm)
    else:
        y1, y2 = _seq_dft(va, vb, tables_nat, B, nseg, norm)
        f = _twiddle_combine(y1, y2, norm)
    return f.reshape(G, N_PLANES, PLANE_ROWS, fw)


def _out_proj_kernel(o1_ref, o4_ref, o16_ref, l1_ref, l4_ref, l16_ref, f_ref, x_ref, mod_ref,
                     ga_ref, gf_ref, gpm_ref, gpl_ref, e2_ref, w_ref, x1_ref, h2_ref,
                     xslab, wcat, wbc, mixed, ybuf, *, tm):
    aw = ATTN_WIDTH
    pr = tm // N_PLANES
    for s in range(D_SLABS):
        xslab[s] = x_ref[:, _lane_slab(s)]

    for r in range(N_PLANES):
        rows = slice(r * pr, (r + 1) * pr)
        l1, l4, l16 = l1_ref[r], l4_ref[r], l16_ref[r]
        top = jnp.maximum(jnp.maximum(l1, l4), l16)
        e1, e4, e16 = jnp.exp(l1 - top), jnp.exp(l4 - top), jnp.exp(l16 - top)
        inv = 1.0 / (e1 + e4 + e16)
        for b, e in enumerate((e1, e4, e16)):
            w = e * inv
            hi = w.astype(BF16)
            wcat[b, rows, 0:LANES] = hi
            wcat[b, rows, LANES:] = (w - hi.astype(F32)).astype(BF16)
    for b in range(3):
        wbc[b] = jnp.dot(wcat[b], e2_ref[...], preferred_element_type=F32)

    for r in range(N_PLANES):
        rows = slice(r * pr, (r + 1) * pr)
        o16 = jnp.concatenate([o16_ref[r, h] for h in range(N_HEADS)], axis=1)
        a = (wbc[0, rows, :] * o1_ref[r].astype(F32) + wbc[1, rows, :] * o4_ref[r].astype(F32)
             + wbc[2, rows, :] * o16)
        mixed[rows, :aw] = _rms_rows(a, ga_ref[...]).astype(BF16)
        mixed[rows, aw:] = _rms_rows(f_ref[r].astype(F32), gf_ref[...]).astype(BF16)

    ybuf[...] = jnp.dot(mixed[...], w_ref[...], preferred_element_type=F32)

    gate1 = mod_ref[2:3, :]
    shift2 = mod_ref[3:4, :]
    scale2 = 1.0 + mod_ref[4:5, :]
    for r in range(N_PLANES):
        rows = slice(r * pr, (r + 1) * pr)
        xr = jnp.concatenate([xslab[s, pl.ds(r, pr, stride=N_PLANES), :] for s in range(D_SLABS)], axis=1)
        x1 = xr + gate1 * _rms_rows(ybuf[rows, :], gpm_ref[...])
        x1_ref[r] = x1
        h2_ref[r] = (_rms_rows(x1, gpl_ref[...]) * scale2 + shift2).astype(BF16)


def _lane_broadcast_matrix():
    e = np.zeros((2 * LANES, ATTN_WIDTH), np.float32)
    for h in range(N_HEADS):
        e[h, h * HEAD_DIM:(h + 1) * HEAD_DIM] = 1.0
        e[LANES + h, h * HEAD_DIM:(h + 1) * HEAD_DIM] = 1.0
    return jnp.asarray(e, BF16)


def _out_proj(o1, o4, o16, l1, l4, l16, f, xg, mod, nseg, g_attn_out, g_fourier_out, g_post_mix, g_pre_mlp, w_out_bf16):
    G = xg.shape[0]
    tm = 256
    pr = tm // N_PLANES
    aw, fw = ATTN_WIDTH, FOURIER_WIDTH
    plane = lambda g, i: (g, 0, i, 0)
    const2 = lambda g, i: (0, 0)
    return pl.pallas_call(
        functools.partial(_out_proj_kernel, tm=tm),
        grid=(G, SEG // tm),
        in_specs=[
            pl.BlockSpec((None, N_PLANES, pr, aw), plane),
            pl.BlockSpec((None, N_PLANES, pr, aw), plane),
            pl.BlockSpec((None, N_PLANES, N_HEADS, pr, LANES), lambda g, i: (g, 0, 0, i, 0)),
            pl.BlockSpec((None, N_PLANES, pr, LANES), plane),
            pl.BlockSpec((None, N_PLANES, pr, LANES), plane),
            pl.BlockSpec((None, N_PLANES, pr, LANES), plane),
            pl.BlockSpec((None, N_PLANES, pr, fw), plane),
            pl.BlockSpec((None, tm, D_MODEL), lambda g, i: (g, i, 0)),
            pl.BlockSpec((None, N_MOD, D_MODEL), lambda g, i: (g // nseg, 0, 0)),
            pl.BlockSpec((1, aw), const2),
            pl.BlockSpec((1, fw), const2),
            pl.BlockSpec((1, D_MODEL), const2),
            pl.BlockSpec((1, D_MODEL), const2),
            pl.BlockSpec((2 * LANES, aw), const2),
            pl.BlockSpec((aw + fw, D_MODEL), const2),
        ],
        out_specs=[pl.BlockSpec((None, N_PLANES, pr, D_MODEL), plane)] * 2,
        out_shape=[jax.ShapeDtypeStruct((G, N_PLANES, PLANE_ROWS, D_MODEL), F32),
                   jax.ShapeDtypeStruct((G, N_PLANES, PLANE_ROWS, D_MODEL), BF16)],
        scratch_shapes=[
            pltpu.VMEM((D_SLABS, tm, LANES), F32),
            pltpu.VMEM((3, tm, 2 * LANES), BF16),
            pltpu.VMEM((3, tm, aw), F32),
            pltpu.VMEM((tm, aw + fw), BF16),
            pltpu.VMEM((tm, D_MODEL), F32),
        ],
        compiler_params=_params(("parallel", "parallel"), 56),
        name="out_proj",
    )(o1, o4, o16, l1, l4, l16, f, xg, mod, g_attn_out.reshape(1, aw), g_fourier_out.reshape(1, fw),
      g_post_mix.reshape(1, D_MODEL), g_pre_mlp.reshape(1, D_MODEL), _lane_broadcast_matrix(), w_out_bf16)


def _mlp_kernel(h_ref, x1_ref, mod_ref, g_ref, w1_ref, w2_ref, out_ref, acc_ref, oslab, *, tm, n_j):
    j = pl.program_id(2)
    pr = tm // N_PLANES

    @pl.when(j == 0)
    def _():
        acc_ref[...] = jnp.zeros_like(acc_ref)

    h = h_ref[...].reshape(tm, D_MODEL)
    hid = jnp.dot(h, w1_ref[...], preferred_element_type=F32)
    hid = jnp.square(jnp.maximum(hid, 0.0)).astype(BF16)
    acc_ref[...] += jnp.dot(hid, w2_ref[...], preferred_element_type=F32)

    @pl.when(j == n_j - 1)
    def _():
        gate2 = mod_ref[5:6, :]
        rc = 64
        for r in range(N_PLANES):
            def body(c, carry):
                m0 = pl.multiple_of(c * rc, rc)
                y = acc_ref[pl.ds(r * pr + m0, rc), :]
                v = x1_ref[r, pl.ds(m0, rc), :] + gate2 * _rms_rows(y, g_ref[...])
                for s in range(D_SLABS):
                    oslab[s, pl.ds(r + N_PLANES * m0, rc, stride=N_PLANES), :] = v[:, _lane_slab(s)]
                return carry

            lax.fori_loop(0, pr // rc, body, 0)
        for s in range(D_SLABS):
            out_ref[:, _lane_slab(s)] = oslab[s]


def _mlp(h2, x1, mod, nseg, g_post_mlp, w1_bf16, w2_bf16):
    G = x1.shape[0]
    tm, tf = 512, 512
    pr = tm // N_PLANES
    n_j = D_FF // tf
    plane = lambda g, i, j: (g, 0, i, 0)
    return pl.pallas_call(
        functools.partial(_mlp_kernel, tm=tm, n_j=n_j),
        grid=(G, SEG // tm, n_j),
        in_specs=[
            pl.BlockSpec((None, N_PLANES, pr, D_MODEL), plane),
            pl.BlockSpec((None, N_PLANES, pr, D_MODEL), plane),
            pl.BlockSpec((None, N_MOD, D_MODEL), lambda g, i, j: (g // nseg, 0, 0)),
            pl.BlockSpec((1, D_MODEL), lambda g, i, j: (0, 0)),
            pl.BlockSpec((D_MODEL, tf), lambda g, i, j: (0, j)),
            pl.BlockSpec((tf, D_MODEL), lambda g, i, j: (j, 0)),
        ],
        out_specs=pl.BlockSpec((None, tm, D_MODEL), lambda g, i, j: (g, i, 0)),
        out_shape=jax.ShapeDtypeStruct((G, SEG, D_MODEL), F32),
        scratch_shapes=[pltpu.VMEM((tm, D_MODEL), F32), pltpu.VMEM((D_SLABS, tm, LANES), F32)],
        compiler_params=_params(("parallel", "parallel", "arbitrary"), 48),
        name="mlp",
    )(h2, x1, mod, g_post_mlp.reshape(1, D_MODEL), w1_bf16, w2_bf16)


def _layer(x, mod, p):
    B, S, _ = x.shape
    assert S % SEG == 0
    nseg = S // SEG
    G = B * nseg
    xg = x.reshape(G, SEG, D_MODEL)
    z4, z16 = _in_proj(xg, mod, nseg, p["g_pre_mix"], p["w_in"])
    z4s = z4.reshape(B, nseg, N_PLANES, PLANE_ROWS, IN_COLS)
    z16s = z16.reshape(B, nseg, 4, N_PLANES, CLASS_ROWS, 3 * ATTN_WIDTH)
    o1, l1 = _attn_d1(z4s, B, nseg)
    o4, l4 = _attn_d4(z4s, B, nseg)
    o16, l16 = _attn_d16(z16s, B, nseg)
    f = _fourier(z4, p["wcs"], p["tables_nat"], p["tables_plane"], B, nseg)
    pshape = (G, N_PLANES, PLANE_ROWS)
    x1, h2 = _out_proj(
        o1.reshape(pshape + (ATTN_WIDTH,)), o4.reshape(pshape + (ATTN_WIDTH,)),
        o16.reshape((G, N_PLANES, N_HEADS, PLANE_ROWS, LANES)),
        l1.reshape(pshape + (LANES,)), l4.reshape(pshape + (LANES,)), l16.reshape(pshape + (LANES,)),
        f, xg, mod, nseg, p["g_attn_out"], p["g_fourier_out"], p["g_post_mix"], p["g_pre_mlp"], p["w_out"])
    out = _mlp(h2, x1, mod, nseg, p["g_post_mlp"], p["w_mlp_in"], p["w_mlp_out"])
    return out.reshape(B, S, D_MODEL)


def kernel(x_prompt, x_sample, c_prompt, c_sample, w_ada, b_ada, g_pre_mix, w_in, g_attn_out, w_fourier,
           g_fourier_out, w_out, g_post_mix, g_pre_mlp, w_mlp_in, w_mlp_out, g_post_mlp):
    depth = w_ada.shape[0]
    nb_p, nb_s = c_prompt.shape[0], c_sample.shape[0]
    pad = (-(nb_p + nb_s)) % 8
    c_all = jnp.concatenate([c_prompt, c_sample, jnp.zeros((pad, D_MODEL), F32)], axis=0)
    tables_nat, tables_plane = _dft_tables()
    xp, xs = x_prompt, x_sample
    n_qkv = 3 * ATTN_WIDTH
    for l in range(depth):
        mod = _modulation(c_all, w_ada[l], b_ada[l])
        mod_p = mod[:nb_p].reshape(nb_p, N_MOD, D_MODEL)
        mod_s = mod[nb_p:nb_p + nb_s].reshape(nb_s, N_MOD, D_MODEL)
        w_in_perm = jnp.concatenate([w_in[l][:, n_qkv:], w_in[l][:, :n_qkv]], axis=1).astype(BF16)
        p = dict(
            g_pre_mix=g_pre_mix[l], w_in=w_in_perm, g_attn_out=g_attn_out[l],
            g_fourier_out=g_fourier_out[l], w_out=w_out[l].astype(BF16), g_post_mix=g_post_mix[l],
            g_pre_mlp=g_pre_mlp[l], w_mlp_in=w_mlp_in[l].astype(BF16), w_mlp_out=w_mlp_out[l].astype(BF16),
            g_post_mlp=g_post_mlp[l], wcs=_fold_channel_dft(w_fourier[l]),
            tables_nat=tables_nat, tables_plane=tables_plane,
        )
        xp = _layer(xp, mod_p, p)
        xs = _layer(xs, mod_s, p)
    return (xp, xs)
```

```python
import functools
import math

import numpy as np
import jax
import jax.numpy as jnp
from jax import lax
from jax.experimental import pallas as pl
from jax.experimental.pallas import tpu as pltpu

D_MODEL = 2048
ATTN_WIDTH = 1024
FOURIER_WIDTH = 1024
HEAD_DIM = 128
N_HEADS = 8
N_GROUPS = 4
GROUP_DIM = 256
D_FF = 8192
IN_COLS = 3 * ATTN_WIDTH + FOURIER_WIDTH
DILATED_BRANCHES = ((128, 1), (512, 4), (2048, 16))
HALF = 64
N_MOD = 6
RMS_EPS = 1e-6
NEG_INF = -1e30

SEG = 2048
N_PLANES = 4
PLANE_ROWS = SEG // N_PLANES
N_CLASSES = 16
CLASS_ROWS = SEG // N_CLASSES
LANES = 128
D_SLABS = D_MODEL // LANES
Q_SUB = 128
K_SUB = Q_SUB + 2 * HALF

F32 = jnp.float32
BF16 = jnp.bfloat16
MIB = 1024 * 1024

assert DILATED_BRANCHES == ((128, 1), (512, 4), (2048, 16))
assert all(w // (2 * d) == HALF for w, d in DILATED_BRANCHES)


def _params(semantics, vmem_mib):
    return pltpu.CompilerParams(dimension_semantics=semantics, vmem_limit_bytes=vmem_mib * MIB)


def _rms_rows(v, gain):
    ms = jnp.mean(v * v, axis=-1, keepdims=True)
    return v * lax.rsqrt(ms + RMS_EPS) * gain


def _lane_slab(s):
    return slice(s * LANES, (s + 1) * LANES)


def _mod_kernel(c_ref, w_ref, b_ref, o_ref):
    c = c_ref[...]
    a = (c * jax.nn.sigmoid(c)).astype(BF16)
    o_ref[...] = jnp.dot(a, w_ref[...].astype(BF16), preferred_element_type=F32) + b_ref[...]


def _modulation(c_all, w_ada, b_ada):
    rows = c_all.shape[0]
    n = w_ada.shape[1]
    tn = 1024
    return pl.pallas_call(
        _mod_kernel,
        grid=(n // tn,),
        in_specs=[
            pl.BlockSpec((rows, D_MODEL), lambda j: (0, 0)),
            pl.BlockSpec((D_MODEL, tn), lambda j: (0, j)),
            pl.BlockSpec((1, tn), lambda j: (0, j)),
        ],
        out_specs=pl.BlockSpec((rows, tn), lambda j: (0, j)),
        out_shape=jax.ShapeDtypeStruct((rows, n), F32),
        compiler_params=_params(("parallel",), 40),
        name="mod",
    )(c_all, w_ada, b_ada.reshape(1, n))


def _in_proj_kernel(x_ref, mod_ref, g_ref, w_ref, z4_ref, z16_ref, h_ref, hslab, aslab, *, tm, tn, nb):
    j = pl.program_id(2)
    pr = tm // N_PLANES
    pb = nb // N_PLANES

    @pl.when(j == 0)
    def _():
        shift = mod_ref[0:1, :]
        scale1 = 1.0 + mod_ref[1:2, :]
        gain = g_ref[...]

        def block(c, carry):
            n0 = pl.multiple_of(c * nb, nb)
            for q in range(nb // 64):
                h = _rms_rows(x_ref[pl.ds(n0 + q * 64, 64), :], gain) * scale1 + shift
                for s in range(D_SLABS):
                    hslab[s, q * 64:(q + 1) * 64, :] = h[:, _lane_slab(s)]
            m0 = pl.multiple_of(c * pb, pb)
            for r in range(N_PLANES):
                for s in range(D_SLABS):
                    h_ref[pl.ds(r * pr + m0, pb), _lane_slab(s)] = (
                        hslab[s, pl.ds(r, pb, stride=N_PLANES), :].astype(BF16))
            return carry

        lax.fori_loop(0, tm // nb, block, 0)

    acc = jnp.dot(h_ref[...], w_ref[...], preferred_element_type=F32)
    acc = acc * jnp.where(j == 1, HEAD_DIM ** -0.5, 1.0).astype(F32)
    for r in range(N_PLANES):
        z4_ref[r] = acc[r * pr:(r + 1) * pr].astype(BF16)
    for s in range(tn // LANES):
        aslab[s] = acc[:, _lane_slab(s)]
    cr = pr // 4
    for r in range(N_PLANES):
        for a in range(4):
            for s in range(tn // LANES):
                z16_ref[r + 4 * a, :, _lane_slab(s)] = (
                    aslab[s, pl.ds(r * pr + a, cr, stride=4), :].astype(BF16))


def _in_proj(xg, mod, nseg, g_pre_mix, w_in_bf16):
    G = xg.shape[0]
    tm, tn = 1024, 1024
    n_j = IN_COLS // tn
    return pl.pallas_call(
        functools.partial(_in_proj_kernel, tm=tm, tn=tn, nb=256),
        grid=(G, SEG // tm, n_j),
        in_specs=[
            pl.BlockSpec((None, tm, D_MODEL), lambda g, i, j: (g, i, 0)),
            pl.BlockSpec((None, N_MOD, D_MODEL), lambda g, i, j: (g // nseg, 0, 0)),
            pl.BlockSpec((1, D_MODEL), lambda g, i, j: (0, 0)),
            pl.BlockSpec((D_MODEL, tn), lambda g, i, j: (0, (j + n_j - 1) % n_j)),
        ],
        out_specs=[
            pl.BlockSpec((None, N_PLANES, tm // N_PLANES, tn), lambda g, i, j: (g, 0, i, (j + n_j - 1) % n_j)),
            pl.BlockSpec((None, N_CLASSES, tm // N_CLASSES, tn), lambda g, i, j: (g, 0, i, jnp.maximum(j, 1) - 1)),
        ],
        out_shape=[
            jax.ShapeDtypeStruct((G, N_PLANES, PLANE_ROWS, IN_COLS), BF16),
            jax.ShapeDtypeStruct((G, N_CLASSES, CLASS_ROWS, 3 * ATTN_WIDTH), BF16),
        ],
        scratch_shapes=[
            pltpu.VMEM((tm, D_MODEL), BF16),
            pltpu.VMEM((D_SLABS, 256, LANES), F32),
            pltpu.VMEM((tn // LANES, tm, LANES), F32),
        ],
        compiler_params=_params(("parallel", "parallel", "arbitrary"), 56),
        name="in_proj",
    )(xg, mod, g_pre_mix.reshape(1, D_MODEL), w_in_bf16)


def _softmax_pv(s, vh):
    m = jnp.max(s, axis=-1, keepdims=True)
    p = jnp.exp(s - m)
    den = jnp.sum(p, axis=-1, keepdims=True)
    o = jnp.dot(p.astype(BF16), vh, preferred_element_type=F32) / den
    return o, m + jnp.log(den)


def _scores(qh, kh, bias):
    return lax.dot_general(qh, kh, (((1,), (1,)), ((), ())), preferred_element_type=F32) + bias


def _band_bias(dil):
    slopes = 2.0 ** (-8.0 * (np.arange(N_HEADS, dtype=np.float64) + 1.0) / N_HEADS)
    rel = np.abs((np.arange(K_SUB)[None, :] - HALF) - np.arange(Q_SUB)[:, None]).astype(np.float64)
    bias = -(slopes * dil)[:, None, None] * rel[None]
    return jnp.asarray(np.where(rel[None] <= HALF, bias, NEG_INF), dtype=F32)


def _plane_bias():
    slopes = 2.0 ** (-8.0 * (np.arange(N_HEADS, dtype=np.float64) + 1.0) / N_HEADS)
    qr, qm = np.divmod(np.arange(Q_SUB), Q_SUB // N_PLANES)
    kr, km = np.divmod(np.arange(K_SUB), K_SUB // N_PLANES)
    rel = np.abs(N_PLANES * (km[None, :] - HALF // N_PLANES - qm[:, None]) + (kr[None, :] - qr[:, None])).astype(np.float64)
    bias = -slopes[:, None, None] * rel[None]
    return jnp.asarray(np.where(rel[None] <= HALF, bias, NEG_INF), dtype=F32)


def _attn_class_kernel(q_ref, kp_ref, kc_ref, kn_ref, vp_ref, vc_ref, vn_ref, bias_ref,
                       o_ref, lse_ref, kext, vext, *, lq, n_i):
    i = pl.program_id(2)
    kext[0:HALF, :] = kp_ref[...]
    kext[HALF:HALF + lq, :] = kc_ref[...]
    kext[HALF + lq:, :] = kn_ref[...]
    vext[0:HALF, :] = vp_ref[...]
    vext[HALF:HALF + lq, :] = vc_ref[...]
    vext[HALF + lq:, :] = vn_ref[...]

    n_sub = lq // Q_SUB
    col = lax.broadcasted_iota(jnp.int32, (Q_SUB, K_SUB), 1)
    lane = lax.broadcasted_iota(jnp.int32, (Q_SUB, LANES), 1)
    for sb in range(n_sub):
        r0 = sb * Q_SUB
        lse_tile = jnp.zeros((Q_SUB, LANES), F32)
        for h in range(N_HEADS):
            hc = _lane_slab(h)
            s = _scores(q_ref[r0:r0 + Q_SUB, hc], kext[r0:r0 + K_SUB, hc], bias_ref[h])
            if sb == 0:
                s = jnp.where((col >= HALF) | (i > 0), s, NEG_INF)
            if sb == n_sub - 1:
                s = jnp.where((col < Q_SUB + HALF) | (i < n_i - 1), s, NEG_INF)
            o, lse = _softmax_pv(s, vext[r0:r0 + K_SUB, hc])
            o_ref[r0:r0 + Q_SUB, hc] = o.astype(BF16)
            lse_tile = jnp.where(lane == h, lse, lse_tile)
        lse_ref[r0:r0 + Q_SUB, :] = lse_tile


def _attn_d4(z4, B, nseg):
    aw = ATTN_WIDTH
    lq = PLANE_ROWS
    hb = lq // HALF

    def cur(part):
        return pl.BlockSpec((None, None, None, lq, aw), lambda b, r, i: (b, i, r, 0, part))

    def prev(part):
        return pl.BlockSpec((None, None, None, HALF, aw), lambda b, r, i: (b, jnp.maximum(i - 1, 0), r, hb - 1, part))

    def nxt(part):
        return pl.BlockSpec((None, None, None, HALF, aw), lambda b, r, i: (b, jnp.minimum(i + 1, nseg - 1), r, 0, part))

    return pl.pallas_call(
        functools.partial(_attn_class_kernel, lq=lq, n_i=nseg),
        grid=(B, N_PLANES, nseg),
        in_specs=[cur(0), prev(1), cur(1), nxt(1), prev(2), cur(2), nxt(2),
                  pl.BlockSpec((N_HEADS, Q_SUB, K_SUB), lambda b, r, i: (0, 0, 0))],
        out_specs=[
            pl.BlockSpec((None, None, None, lq, aw), lambda b, r, i: (b, i, r, 0, 0)),
            pl.BlockSpec((None, None, None, lq, LANES), lambda b, r, i: (b, i, r, 0, 0)),
        ],
        out_shape=[
            jax.ShapeDtypeStruct((B, nseg, N_PLANES, PLANE_ROWS, aw), BF16),
            jax.ShapeDtypeStruct((B, nseg, N_PLANES, PLANE_ROWS, LANES), F32),
        ],
        scratch_shapes=[pltpu.VMEM((lq + 2 * HALF, aw), BF16), pltpu.VMEM((lq + 2 * HALF, aw), BF16)],
        compiler_params=_params(("parallel", "parallel", "arbitrary"), 32),
        name="attn_d4",
    )(z4, z4, z4, z4, z4, z4, z4, _band_bias(4))


def _attn_d16_kernel(q_ref, kp_ref, kc_ref, kn_ref, vp_ref, vc_ref, vn_ref, bias_ref, o_ref, lse_ref, *, n_i):
    i = pl.program_id(2)
    col = lax.broadcasted_iota(jnp.int32, (Q_SUB, K_SUB), 1)
    lane = lax.broadcasted_iota(jnp.int32, (Q_SUB, LANES), 1)
    for a in range(4):
        lse_tile = jnp.zeros((Q_SUB, LANES), F32)
        for h in range(N_HEADS):
            hc = _lane_slab(h)
            kh = jnp.concatenate([kp_ref[a, :, hc], kc_ref[a, :, hc], kn_ref[a, :, hc]], axis=0)
            vh = jnp.concatenate([vp_ref[a, :, hc], vc_ref[a, :, hc], vn_ref[a, :, hc]], axis=0)
            s = _scores(q_ref[a, :, hc], kh, bias_ref[h])
            s = jnp.where((col >= HALF) | (i > 0), s, NEG_INF)
            s = jnp.where((col < Q_SUB + HALF) | (i < n_i - 1), s, NEG_INF)
            o, lse = _softmax_pv(s, vh)
            o_ref[h, pl.ds(a, CLASS_ROWS, stride=4), :] = o
            lse_tile = jnp.where(lane == h, lse, lse_tile)
        lse_ref[pl.ds(a, CLASS_ROWS, stride=4), :] = lse_tile


def _attn_d16(z16, B, nseg):
    aw = ATTN_WIDTH
    lq = CLASS_ROWS
    assert lq == Q_SUB
    hb = lq // HALF

    def cur(part):
        return pl.BlockSpec((None, None, 4, None, lq, aw), lambda b, r, i: (b, i, 0, r, 0, part))

    def prev(part):
        return pl.BlockSpec((None, None, 4, None, HALF, aw),
                            lambda b, r, i: (b, jnp.maximum(i - 1, 0), 0, r, hb - 1, part))

    def nxt(part):
        return pl.BlockSpec((None, None, 4, None, HALF, aw),
                            lambda b, r, i: (b, jnp.minimum(i + 1, nseg - 1), 0, r, 0, part))

    return pl.pallas_call(
        functools.partial(_attn_d16_kernel, n_i=nseg),
        grid=(B, N_PLANES, nseg),
        in_specs=[cur(0), prev(1), cur(1), nxt(1), prev(2), cur(2), nxt(2),
                  pl.BlockSpec((N_HEADS, Q_SUB, K_SUB), lambda b, r, i: (0, 0, 0))],
        out_specs=[
            pl.BlockSpec((None, None, None, N_HEADS, PLANE_ROWS, LANES), lambda b, r, i: (b, i, r, 0, 0, 0)),
            pl.BlockSpec((None, None, None, PLANE_ROWS, LANES), lambda b, r, i: (b, i, r, 0, 0)),
        ],
        out_shape=[
            jax.ShapeDtypeStruct((B, nseg, N_PLANES, N_HEADS, PLANE_ROWS, LANES), F32),
            jax.ShapeDtypeStruct((B, nseg, N_PLANES, PLANE_ROWS, LANES), F32),
        ],
        compiler_params=_params(("parallel", "parallel", "arbitrary"), 32),
        name="attn_d16",
    )(z16, z16, z16, z16, z16, z16, z16, _band_bias(16))


def _attn_d1_kernel(q_ref, kp_ref, kc_ref, kn_ref, vp_ref, vc_ref, vn_ref, bias_ref,
                    o_ref, lse_ref, kext, vext, *, mq, n_t):
    t = pl.program_id(1)
    qm = Q_SUB // N_PLANES
    km = K_SUB // N_PLANES
    hm = HALF // N_PLANES
    for r in range(N_PLANES):
        kext[r, 0:hm, :] = kp_ref[r]
        kext[r, hm:hm + mq, :] = kc_ref[r]
        kext[r, hm + mq:, :] = kn_ref[r]
        vext[r, 0:hm, :] = vp_ref[r]
        vext[r, hm:hm + mq, :] = vc_ref[r]
        vext[r, hm + mq:, :] = vn_ref[r]

    n_sub = mq // qm
    colm = lax.broadcasted_iota(jnp.int32, (Q_SUB, K_SUB), 1) & (km - 1)
    lane = lax.broadcasted_iota(jnp.int32, (Q_SUB, LANES), 1)
    for sb in range(n_sub):
        m1 = sb * qm
        lse_tile = jnp.zeros((Q_SUB, LANES), F32)
        for h in range(N_HEADS):
            hc = _lane_slab(h)
            qh = jnp.concatenate([q_ref[r, m1:m1 + qm, hc] for r in range(N_PLANES)], axis=0)
            kh = jnp.concatenate([kext[r, m1:m1 + km, hc] for r in range(N_PLANES)], axis=0)
            vh = jnp.concatenate([vext[r, m1:m1 + km, hc] for r in range(N_PLANES)], axis=0)
            s = _scores(qh, kh, bias_ref[h])
            if sb == 0:
                s = jnp.where((colm >= hm) | (t > 0), s, NEG_INF)
            if sb == n_sub - 1:
                s = jnp.where((colm < km - hm) | (t < n_t - 1), s, NEG_INF)
            o, lse = _softmax_pv(s, vh)
            for r in range(N_PLANES):
                o_ref[r, m1:m1 + qm, hc] = o[r * qm:(r + 1) * qm].astype(BF16)
            lse_tile = jnp.where(lane == h, lse, lse_tile)
        for r in range(N_PLANES):
            lse_ref[r, m1:m1 + qm, :] = lse_tile[r * qm:(r + 1) * qm]


def _attn_d1(z4, B, nseg):
    aw = ATTN_WIDTH
    mq = 128
    hm = HALF // N_PLANES
    per_seg = PLANE_ROWS // mq
    hb_step = mq // hm
    hb_seg = PLANE_ROWS // hm
    n_t = nseg * per_seg

    def cur(part):
        return pl.BlockSpec((None, None, N_PLANES, mq, aw), lambda b, t: (b, t // per_seg, 0, t % per_seg, part))

    def prev(part):
        def imap(b, t):
            q = jnp.maximum(t * hb_step - 1, 0)
            return (b, q // hb_seg, 0, q % hb_seg, part)
        return pl.BlockSpec((None, None, N_PLANES, hm, aw), imap)

    def nxt(part):
        def imap(b, t):
            q = jnp.minimum((t + 1) * hb_step, nseg * hb_seg - 1)
            return (b, q // hb_seg, 0, q % hb_seg, part)
        return pl.BlockSpec((None, None, N_PLANES, hm, aw), imap)

    return pl.pallas_call(
        functools.partial(_attn_d1_kernel, mq=mq, n_t=n_t),
        grid=(B, n_t),
        in_specs=[cur(0), prev(1), cur(1), nxt(1), prev(2), cur(2), nxt(2),
                  pl.BlockSpec((N_HEADS, Q_SUB, K_SUB), lambda b, t: (0, 0, 0))],
        out_specs=[
            pl.BlockSpec((None, None, N_PLANES, mq, aw), lambda b, t: (b, t // per_seg, 0, t % per_seg, 0)),
            pl.BlockSpec((None, None, N_PLANES, mq, LANES), lambda b, t: (b, t // per_seg, 0, t % per_seg, 0)),
        ],
        out_shape=[
            jax.ShapeDtypeStruct((B, nseg, N_PLANES, PLANE_ROWS, aw), BF16),
            jax.ShapeDtypeStruct((B, nseg, N_PLANES, PLANE_ROWS, LANES), F32),
        ],
        scratch_shapes=[pltpu.VMEM((N_PLANES, mq + 2 * hm, aw), BF16), pltpu.VMEM((N_PLANES, mq + 2 * hm, aw), BF16)],
        compiler_params=_params(("parallel", "arbitrary"), 32),
        name="attn_d1",
    )(z4, z4, z4, z4, z4, z4, z4, _plane_bias())


def _dft_tables():
    n = SEG
    n1 = 32
    p_per = n1 // N_PLANES
    cols_nat = np.arange(n, dtype=np.float64)
    cols_plane = (N_PLANES * (np.arange(n) % PLANE_ROWS) + np.arange(n) // PLANE_ROWS).astype(np.float64)

    def build(cols):
        a = 2.0 * np.pi * ((np.arange(n // n1)[:, None] * n1 * cols[None, :]) % n) / n
        jr = (N_PLANES * np.arange(p_per)[None, :] + np.arange(N_PLANES)[:, None]).astype(np.float64)
        b = 2.0 * np.pi * ((jr[:, :, None] * cols[None, None, :]) % n) / n
        ca, sa = jnp.asarray(np.cos(a), F32)[None, :, None, :], jnp.asarray(np.sin(a), F32)[None, :, None, :]
        cb, sb = jnp.asarray(np.cos(b), F32)[:, None, :, :], jnp.asarray(np.sin(b), F32)[:, None, :, :]
        cos = (ca * cb - sa * sb).reshape(n, n).astype(BF16)
        msin = (-(sa * cb + ca * sb)).reshape(n, n).astype(BF16)
        return cos, msin

    return build(cols_nat), build(cols_plane)


def _fold_kernel(c_ref, s_ref, w_ref, o_ref):
    w = w_ref[...]
    o_ref[:, :GROUP_DIM] = jnp.dot(c_ref[...], w, preferred_element_type=F32,
                                   precision=lax.Precision.HIGHEST).astype(BF16)
    o_ref[:, GROUP_DIM:] = jnp.dot(s_ref[...], w, preferred_element_type=F32,
                                   precision=lax.Precision.HIGHEST).astype(BF16)


def _fold_channel_dft(w_fourier):
    n = GROUP_DIM
    idx = (np.arange(n)[:, None] * np.arange(n)[None, :]) % n
    ang = 2.0 * np.pi * idx / n
    cos_c = jnp.asarray(np.cos(ang), F32)
    sin_c = jnp.asarray(np.sin(ang), F32)
    return pl.pallas_call(
        _fold_kernel,
        grid=(N_GROUPS,),
        in_specs=[
            pl.BlockSpec((n, n), lambda g: (0, 0)),
            pl.BlockSpec((n, n), lambda g: (0, 0)),
            pl.BlockSpec((None, n, n), lambda g: (g, 0, 0)),
        ],
        out_specs=pl.BlockSpec((None, n, 2 * n), lambda g: (g, 0, 0)),
        out_shape=jax.ShapeDtypeStruct((N_GROUPS, n, 2 * n), BF16),
        compiler_params=_params(("parallel",), 16),
        name="fold_channel_dft",
    )(cos_c, sin_c, w_fourier)


def _seq_dft_kernel(c_ref, ms_ref, u_ref, w_ref, *refs, norm, both):
    n_out = 2 if both else 1
    out_refs, (va_ref, vb_ref) = refs[:n_out], refs[n_out:]
    i = pl.program_id(2)

    @pl.when(i == 0)
    def _():
        for blk in range(SEG // PLANE_ROWS):
            rows = slice(blk * PLANE_ROWS, (blk + 1) * PLANE_ROWS)
            for g in range(N_GROUPS):
                cols = slice(g * GROUP_DIM, (g + 1) * GROUP_DIM)
                v = jnp.dot(u_ref[blk, :, cols], w_ref[g], preferred_element_type=F32)
                va_ref[rows, cols] = v[:, :GROUP_DIM].astype(BF16)
                vb_ref[rows, cols] = v[:, GROUP_DIM:].astype(BF16)

    c = c_ref[...]
    ms = ms_ref[...]
    va = va_ref[...]
    vb = vb_ref[...]
    y1 = jnp.dot(c, va, preferred_element_type=F32) + jnp.dot(ms, vb, preferred_element_type=F32)
    if both:
        y2 = jnp.dot(c, vb, preferred_element_type=F32) - jnp.dot(ms, va, preferred_element_type=F32)
        out_refs[0][...] = y1
        out_refs[1][...] = y2
    else:
        out_refs[0][...] = (y1 * norm).astype(BF16)


def _seq_dft(z4, wcs, tables, B, nseg, norm):
    fw = FOURIER_WIDTH
    tm = 512
    both = nseg > 1
    u_col = IN_COLS // fw - 1
    if both:
        assert nseg == N_PLANES
        n_cls = N_PLANES
        u_spec = pl.BlockSpec((None, nseg, None, PLANE_ROWS, fw), lambda b, s, i: (b, 0, s, 0, u_col))
    else:
        n_cls = 1
        u_spec = pl.BlockSpec((None, None, N_PLANES, PLANE_ROWS, fw), lambda b, s, i: (b, 0, 0, 0, u_col))
    m_spec = pl.BlockSpec((tm, SEG), lambda b, s, i: (i, 0))
    o_spec = pl.BlockSpec((None, None, tm, fw), lambda b, s, i: (b, s, i, 0))
    odt = F32 if both else BF16
    outs = pl.pallas_call(
        functools.partial(_seq_dft_kernel, norm=norm, both=both),
        grid=(B, n_cls, SEG // tm),
        in_specs=[m_spec, m_spec, u_spec,
                  pl.BlockSpec((N_GROUPS, GROUP_DIM, 2 * GROUP_DIM), lambda b, s, i: (0, 0, 0))],
        out_specs=[o_spec] * (2 if both else 1),
        out_shape=[jax.ShapeDtypeStruct((B, n_cls, SEG, fw), odt)] * (2 if both else 1),
        scratch_shapes=[pltpu.VMEM((SEG, fw), BF16), pltpu.VMEM((SEG, fw), BF16)],
        compiler_params=_params(("parallel", "parallel", "arbitrary"), 48),
        name="four_seq_r4" if both else "four_seq_r1",
    )(tables[0], tables[1], z4, wcs)
    return outs if both else outs[0]


def _twiddle_kernel(y1_ref, y2_ref, tw_ref, f_ref, *, radix, norm):
    tw = tw_ref[...]
    acc = None
    for s in range(radix):
        term = tw[:, s:s + 1] * y1_ref[s] - tw[:, radix + s:radix + s + 1] * y2_ref[s]
        acc = term if acc is None else acc + term
    f_ref[...] = (acc * norm).astype(BF16)


def _twiddle_combine(y1, y2, norm):
    B, radix, _, fw = y1.shape
    nseg = radix
    S = nseg * SEG
    tm = 256
    g, r, m = np.meshgrid(np.arange(nseg), np.arange(N_PLANES), np.arange(PLANE_ROWS), indexing="ij")
    j = (SEG * g + N_PLANES * m + r).astype(np.float64)[..., None]
    ang = 2.0 * np.pi * ((j * np.arange(radix)) % S) / S
    tw = np.zeros((nseg, N_PLANES, PLANE_ROWS, LANES), np.float32)
    tw[..., :radix] = np.cos(ang)
    tw[..., radix:2 * radix] = np.sin(ang)
    nb = PLANE_ROWS // tm
    y_spec = pl.BlockSpec((None, radix, tm, fw), lambda b, r, i, g: (b, 0, r * nb + i, 0))
    return pl.pallas_call(
        functools.partial(_twiddle_kernel, radix=radix, norm=norm),
        grid=(B, N_PLANES, nb, nseg),
        in_specs=[y_spec, y_spec, pl.BlockSpec((None, None, tm, LANES), lambda b, r, i, g: (g, r, i, 0))],
        out_specs=pl.BlockSpec((None, None, None, tm, fw), lambda b, r, i, g: (b, g, r, i, 0)),
        out_shape=jax.ShapeDtypeStruct((B, nseg, N_PLANES, PLANE_ROWS, fw), BF16),
        compiler_params=_params(("parallel", "parallel", "parallel", "arbitrary"), 48),
        name="four_twiddle",
    )(y1, y2, jnp.asarray(tw))


def _fourier(z4s, wcs, tables_nat, tables_plane, B, nseg):
    G = B * nseg
    fw = FOURIER_WIDTH
    norm = 1.0 / math.sqrt(nseg * SEG * GROUP_DIM)
    if nseg == 1:
        f = _seq_dft(z4s, wcs, tables_plane, B, nseg, norm)
    else:
        y1, y2 = _seq_dft(z4s, wcs, tables_nat, B, nseg, norm)
        f = _twiddle_combine(y1, y2, norm)
    return f.reshape(G, N_PLANES, PLANE_ROWS, fw)


def _out_proj_kernel(o1_ref, o4_ref, o16_ref, l1_ref, l4_ref, l16_ref, f_ref, x_ref, mod_ref,
                     ga_ref, gf_ref, gpm_ref, gpl_ref, e2_ref, w_ref, x1_ref, h2_ref,
                     xslab, wcat, wbc, mixed, ybuf, *, tm):
    aw = ATTN_WIDTH
    pr = tm // N_PLANES
    for s in range(D_SLABS):
        xslab[s] = x_ref[:, _lane_slab(s)]

    for r in range(N_PLANES):
        rows = slice(r * pr, (r + 1) * pr)
        l1, l4, l16 = l1_ref[r], l4_ref[r], l16_ref[r]
        top = jnp.maximum(jnp.maximum(l1, l4), l16)
        e1, e4, e16 = jnp.exp(l1 - top), jnp.exp(l4 - top), jnp.exp(l16 - top)
        inv = 1.0 / (e1 + e4 + e16)
        for b, e in enumerate((e1, e4, e16)):
            w = e * inv
            hi = w.astype(BF16)
            wcat[b, rows, 0:LANES] = hi
            wcat[b, rows, LANES:] = (w - hi.astype(F32)).astype(BF16)
    for b in range(3):
        wbc[b] = jnp.dot(wcat[b], e2_ref[...], preferred_element_type=F32)

    for r in range(N_PLANES):
        rows = slice(r * pr, (r + 1) * pr)
        o16 = jnp.concatenate([o16_ref[r, h] for h in range(N_HEADS)], axis=1)
        a = (wbc[0, rows, :] * o1_ref[r].astype(F32) + wbc[1, rows, :] * o4_ref[r].astype(F32)
             + wbc[2, rows, :] * o16)
        mixed[rows, :aw] = _rms_rows(a, ga_ref[...]).astype(BF16)
        mixed[rows, aw:] = _rms_rows(f_ref[r].astype(F32), gf_ref[...]).astype(BF16)

    ybuf[...] = jnp.dot(mixed[...], w_ref[...], preferred_element_type=F32)

    gate1 = mod_ref[2:3, :]
    shift2 = mod_ref[3:4, :]
    scale2 = 1.0 + mod_ref[4:5, :]
    for r in range(N_PLANES):
        rows = slice(r * pr, (r + 1) * pr)
        xr = jnp.concatenate([xslab[s, pl.ds(r, pr, stride=N_PLANES), :] for s in range(D_SLABS)], axis=1)
        x1 = xr + gate1 * _rms_rows(ybuf[rows, :], gpm_ref[...])
        x1_ref[r] = x1
        h2_ref[r] = (_rms_rows(x1, gpl_ref[...]) * scale2 + shift2).astype(BF16)


def _lane_broadcast_matrix():
    e = np.zeros((2 * LANES, ATTN_WIDTH), np.float32)
    for h in range(N_HEADS):
        e[h, h * HEAD_DIM:(h + 1) * HEAD_DIM] = 1.0
        e[LANES + h, h * HEAD_DIM:(h + 1) * HEAD_DIM] = 1.0
    return jnp.asarray(e, BF16)


def _out_proj(o1, o4, o16, l1, l4, l16, f, xg, mod, nseg, g_attn_out, g_fourier_out, g_post_mix, g_pre_mlp, w_out_bf16):
    G = xg.shape[0]
    tm = 256
    pr = tm // N_PLANES
    aw, fw = ATTN_WIDTH, FOURIER_WIDTH
    plane = lambda g, i: (g, 0, i, 0)
    const2 = lambda g, i: (0, 0)
    return pl.pallas_call(
        functools.partial(_out_proj_kernel, tm=tm),
        grid=(G, SEG // tm),
        in_specs=[
            pl.BlockSpec((None, N_PLANES, pr, aw), plane),
            pl.BlockSpec((None, N_PLANES, pr, aw), plane),
            pl.BlockSpec((None, N_PLANES, N_HEADS, pr, LANES), lambda g, i: (g, 0, 0, i, 0)),
            pl.BlockSpec((None, N_PLANES, pr, LANES), plane),
            pl.BlockSpec((None, N_PLANES, pr, LANES), plane),
            pl.BlockSpec((None, N_PLANES, pr, LANES), plane),
            pl.BlockSpec((None, N_PLANES, pr, fw), plane),
            pl.BlockSpec((None, tm, D_MODEL), lambda g, i: (g, i, 0)),
            pl.BlockSpec((None, N_MOD, D_MODEL), lambda g, i: (g // nseg, 0, 0)),
            pl.BlockSpec((1, aw), const2),
            pl.BlockSpec((1, fw), const2),
            pl.BlockSpec((1, D_MODEL), const2),
            pl.BlockSpec((1, D_MODEL), const2),
            pl.BlockSpec((2 * LANES, aw), const2),
            pl.BlockSpec((aw + fw, D_MODEL), const2),
        ],
        out_specs=[pl.BlockSpec((None, N_PLANES, pr, D_MODEL), plane)] * 2,
        out_shape=[jax.ShapeDtypeStruct((G, N_PLANES, PLANE_ROWS, D_MODEL), F32),
                   jax.ShapeDtypeStruct((G, N_PLANES, PLANE_ROWS, D_MODEL), BF16)],
        scratch_shapes=[
            pltpu.VMEM((D_SLABS, tm, LANES), F32),
            pltpu.VMEM((3, tm, 2 * LANES), BF16),
            pltpu.VMEM((3, tm, aw), F32),
            pltpu.VMEM((tm, aw + fw), BF16),
            pltpu.VMEM((tm, D_MODEL), F32),
        ],
        compiler_params=_params(("parallel", "parallel"), 56),
        name="out_proj",
    )(o1, o4, o16, l1, l4, l16, f, xg, mod, g_attn_out.reshape(1, aw), g_fourier_out.reshape(1, fw),
      g_post_mix.reshape(1, D_MODEL), g_pre_mlp.reshape(1, D_MODEL), _lane_broadcast_matrix(), w_out_bf16)


def _mlp_kernel(h_ref, x1_ref, mod_ref, g_ref, w1_ref, w2_ref, out_ref, acc_ref, oslab, *, tm, n_j):
    j = pl.program_id(2)
    pr = tm // N_PLANES

    @pl.when(j == 0)
    def _():
        acc_ref[...] = jnp.zeros_like(acc_ref)

    h = h_ref[...].reshape(tm, D_MODEL)
    hid = jnp.dot(h, w1_ref[...], preferred_element_type=F32)
    hid = jnp.square(jnp.maximum(hid, 0.0)).astype(BF16)
    acc_ref[...] += jnp.dot(hid, w2_ref[...], preferred_element_type=F32)

    @pl.when(j == n_j - 1)
    def _():
        gate2 = mod_ref[5:6, :]
        rc = 64
        for r in range(N_PLANES):
            def body(c, carry):
                m0 = pl.multiple_of(c * rc, rc)
                y = acc_ref[pl.ds(r * pr + m0, rc), :]
                v = x1_ref[r, pl.ds(m0, rc), :] + gate2 * _rms_rows(y, g_ref[...])
                for s in range(D_SLABS):
                    oslab[s, pl.ds(r + N_PLANES * m0, rc, stride=N_PLANES), :] = v[:, _lane_slab(s)]
                return carry

            lax.fori_loop(0, pr // rc, body, 0)
        for s in range(D_SLABS):
            out_ref[:, _lane_slab(s)] = oslab[s]


def _mlp(h2, x1, mod, nseg, g_post_mlp, w1_bf16, w2_bf16):
    G = x1.shape[0]
    tm, tf = 512, 1024
    pr = tm // N_PLANES
    n_j = D_FF // tf
    plane = lambda g, i, j: (g, 0, i, 0)
    return pl.pallas_call(
        functools.partial(_mlp_kernel, tm=tm, n_j=n_j),
        grid=(G, SEG // tm, n_j),
        in_specs=[
            pl.BlockSpec((None, N_PLANES, pr, D_MODEL), plane),
            pl.BlockSpec((None, N_PLANES, pr, D_MODEL), plane),
            pl.BlockSpec((None, N_MOD, D_MODEL), lambda g, i, j: (g // nseg, 0, 0)),
            pl.BlockSpec((1, D_MODEL), lambda g, i, j: (0, 0)),
            pl.BlockSpec((D_MODEL, tf), lambda g, i, j: (0, j)),
            pl.BlockSpec((tf, D_MODEL), lambda g, i, j: (j, 0)),
        ],
        out_specs=pl.BlockSpec((None, tm, D_MODEL), lambda g, i, j: (g, i, 0)),
        out_shape=jax.ShapeDtypeStruct((G, SEG, D_MODEL), F32),
        scratch_shapes=[pltpu.VMEM((tm, D_MODEL), F32), pltpu.VMEM((D_SLABS, tm, LANES), F32)],
        compiler_params=_params(("parallel", "parallel", "arbitrary"), 56),
        name="mlp",
    )(h2, x1, mod, g_post_mlp.reshape(1, D_MODEL), w1_bf16, w2_bf16)


def _layer(x, mod, p):
    B, S, _ = x.shape
    assert S % SEG == 0
    nseg = S // SEG
    G = B * nseg
    xg = x.reshape(G, SEG, D_MODEL)
    z4, z16 = _in_proj(xg, mod, nseg, p["g_pre_mix"], p["w_in"])
    z4s = z4.reshape(B, nseg, N_PLANES, PLANE_ROWS, IN_COLS)
    z16s = z16.reshape(B, nseg, 4, N_PLANES, CLASS_ROWS, 3 * ATTN_WIDTH)
    o1, l1 = _attn_d1(z4s, B, nseg)
    o4, l4 = _attn_d4(z4s, B, nseg)
    o16, l16 = _attn_d16(z16s, B, nseg)
    f = _fourier(z4s, p["wcs"], p["tables_nat"], p["tables_plane"], B, nseg)
    pshape = (G, N_PLANES, PLANE_ROWS)
    x1, h2 = _out_proj(
        o1.reshape(pshape + (ATTN_WIDTH,)), o4.reshape(pshape + (ATTN_WIDTH,)),
        o16.reshape((G, N_PLANES, N_HEADS, PLANE_ROWS, LANES)),
        l1.reshape(pshape + (LANES,)), l4.reshape(pshape + (LANES,)), l16.reshape(pshape + (LANES,)),
        f, xg, mod, nseg, p["g_attn_out"], p["g_fourier_out"], p["g_post_mix"], p["g_pre_mlp"], p["w_out"])
    out = _mlp(h2, x1, mod, nseg, p["g_post_mlp"], p["w_mlp_in"], p["w_mlp_out"])
    return out.reshape(B, S, D_MODEL)


def kernel(x_prompt, x_sample, c_prompt, c_sample, w_ada, b_ada, g_pre_mix, w_in, g_attn_out, w_fourier,
           g_fourier_out, w_out, g_post_mix, g_pre_mlp, w_mlp_in, w_mlp_out, g_post_mlp):
    depth = w_ada.shape[0]
    nb_p, nb_s = c_prompt.shape[0], c_sample.shape[0]
    pad = (-(nb_p + nb_s)) % 8
    c_all = jnp.concatenate([c_prompt, c_sample, jnp.zeros((pad, D_MODEL), F32)], axis=0)
    tables_nat, tables_plane = _dft_tables()
    xp, xs = x_prompt, x_sample
    for l in range(depth):
        mod = _modulation(c_all, w_ada[l], b_ada[l])
        mod_p = mod[:nb_p].reshape(nb_p, N_MOD, D_MODEL)
        mod_s = mod[nb_p:nb_p + nb_s].reshape(nb_s, N_MOD, D_MODEL)
        p = dict(
            g_pre_mix=g_pre_mix[l], w_in=w_in[l].astype(BF16), g_attn_out=g_attn_out[l],
            g_fourier_out=g_fourier_out[l], w_out=w_out[l].astype(BF16), g_post_mix=g_post_mix[l],
            g_pre_mlp=g_pre_mlp[l], w_mlp_in=w_mlp_in[l].astype(BF16), w_mlp_out=w_mlp_out[l].astype(BF16),
            g_post_mlp=g_post_mlp[l], wcs=_fold_channel_dft(w_fourier[l]),
            tables_nat=tables_nat, tables_plane=tables_plane,
        )
        xp = _layer(xp, mod_p, p)
        xs = _layer(xs, mod_s, p)
    return (xp, xs)
```

```python
import functools
import math

import numpy as np
import jax
import jax.numpy as jnp
from jax import lax
from jax.experimental import pallas as pl
from jax.experimental.pallas import tpu as pltpu

D_MODEL = 2048
ATTN_WIDTH = 1024
FOURIER_WIDTH = 1024
HEAD_DIM = 128
N_HEADS = 8
N_GROUPS = 4
GROUP_DIM = 256
D_FF = 8192
IN_COLS = 3 * ATTN_WIDTH + FOURIER_WIDTH
DILATED_BRANCHES = ((128, 1), (512, 4), (2048, 16))
HALF = 64
N_MOD = 6
RMS_EPS = 1e-6
NEG_INF = -1e30

SEG = 2048
N_PLANES = 4
PLANE_ROWS = SEG // N_PLANES
N_CLASSES = 16
CLASS_ROWS = SEG // N_CLASSES
LANES = 128
D_SLABS = D_MODEL // LANES
Q_SUB = 128
K_SUB = Q_SUB + 2 * HALF

F32 = jnp.float32
BF16 = jnp.bfloat16
MIB = 1024 * 1024

assert DILATED_BRANCHES == ((128, 1), (512, 4), (2048, 16))
assert all(w // (2 * d) == HALF for w, d in DILATED_BRANCHES)


def _params(semantics, vmem_mib):
    return pltpu.CompilerParams(dimension_semantics=semantics, vmem_limit_bytes=vmem_mib * MIB)


def _rms_rows(v, gain):
    ms = jnp.mean(v * v, axis=-1, keepdims=True)
    return v * lax.rsqrt(ms + RMS_EPS) * gain


def _lane_slab(s):
    return slice(s * LANES, (s + 1) * LANES)


def _mod_kernel(c_ref, w_ref, b_ref, o_ref):
    c = c_ref[...]
    a = (c * jax.nn.sigmoid(c)).astype(BF16)
    o_ref[...] = jnp.dot(a, w_ref[...].astype(BF16), preferred_element_type=F32) + b_ref[...]


def _modulation(c_all, w_ada, b_ada):
    rows = c_all.shape[0]
    n = w_ada.shape[1]
    tn = 1024
    return pl.pallas_call(
        _mod_kernel,
        grid=(n // tn,),
        in_specs=[
            pl.BlockSpec((rows, D_MODEL), lambda j: (0, 0)),
            pl.BlockSpec((D_MODEL, tn), lambda j: (0, j)),
            pl.BlockSpec((1, tn), lambda j: (0, j)),
        ],
        out_specs=pl.BlockSpec((rows, tn), lambda j: (0, j)),
        out_shape=jax.ShapeDtypeStruct((rows, n), F32),
        compiler_params=_params(("parallel",), 40),
        name="mod",
    )(c_all, w_ada, b_ada.reshape(1, n))


def _in_proj_kernel(x_ref, mod_ref, g_ref, w_ref, z4_ref, z16_ref, h_ref, hslab, aslab, *, tm, tn, nb):
    j = pl.program_id(2)
    pr = tm // N_PLANES
    pb = nb // N_PLANES

    @pl.when(j == 0)
    def _():
        shift = mod_ref[0:1, :]
        scale1 = 1.0 + mod_ref[1:2, :]
        gain = g_ref[...]

        def block(c, carry):
            n0 = pl.multiple_of(c * nb, nb)
            for q in range(nb // 64):
                h = _rms_rows(x_ref[pl.ds(n0 + q * 64, 64), :], gain) * scale1 + shift
                for s in range(D_SLABS):
                    hslab[s, q * 64:(q + 1) * 64, :] = h[:, _lane_slab(s)]
            m0 = pl.multiple_of(c * pb, pb)
            for r in range(N_PLANES):
                for s in range(D_SLABS):
                    h_ref[pl.ds(r * pr + m0, pb), _lane_slab(s)] = (
                        hslab[s, pl.ds(r, pb, stride=N_PLANES), :].astype(BF16))
            return carry

        lax.fori_loop(0, tm // nb, block, 0)

    acc = jnp.dot(h_ref[...], w_ref[...], preferred_element_type=F32)
    acc = acc * jnp.where(j == 1, HEAD_DIM ** -0.5, 1.0).astype(F32)
    for r in range(N_PLANES):
        z4_ref[r] = acc[r * pr:(r + 1) * pr].astype(BF16)
    for s in range(tn // LANES):
        aslab[s] = acc[:, _lane_slab(s)]
    cr = pr // 4
    for r in range(N_PLANES):
        for a in range(4):
            for s in range(tn // LANES):
                z16_ref[r + 4 * a, :, _lane_slab(s)] = (
                    aslab[s, pl.ds(r * pr + a, cr, stride=4), :].astype(BF16))


def _in_proj(xg, mod, nseg, g_pre_mix, w_in_bf16):
    G = xg.shape[0]
    tm, tn = 1024, 1024
    n_j = IN_COLS // tn
    return pl.pallas_call(
        functools.partial(_in_proj_kernel, tm=tm, tn=tn, nb=256),
        grid=(G, SEG // tm, n_j),
        in_specs=[
            pl.BlockSpec((None, tm, D_MODEL), lambda g, i, j: (g, i, 0)),
            pl.BlockSpec((None, N_MOD, D_MODEL), lambda g, i, j: (g // nseg, 0, 0)),
            pl.BlockSpec((1, D_MODEL), lambda g, i, j: (0, 0)),
            pl.BlockSpec((D_MODEL, tn), lambda g, i, j: (0, (j + n_j - 1) % n_j)),
        ],
        out_specs=[
            pl.BlockSpec((None, N_PLANES, tm // N_PLANES, tn), lambda g, i, j: (g, 0, i, (j + n_j - 1) % n_j)),
            pl.BlockSpec((None, N_CLASSES, tm // N_CLASSES, tn), lambda g, i, j: (g, 0, i, jnp.maximum(j, 1) - 1)),
        ],
        out_shape=[
            jax.ShapeDtypeStruct((G, N_PLANES, PLANE_ROWS, IN_COLS), BF16),
            jax.ShapeDtypeStruct((G, N_CLASSES, CLASS_ROWS, 3 * ATTN_WIDTH), BF16),
        ],
        scratch_shapes=[
            pltpu.VMEM((tm, D_MODEL), BF16),
            pltpu.VMEM((D_SLABS, 256, LANES), F32),
            pltpu.VMEM((tn // LANES, tm, LANES), F32),
        ],
        compiler_params=_params(("parallel", "parallel", "arbitrary"), 56),
        name="in_proj",
    )(xg, mod, g_pre_mix.reshape(1, D_MODEL), w_in_bf16)


def _softmax_pv(s, vh):
    m = jnp.max(s, axis=-1, keepdims=True)
    p = jnp.exp(s - m)
    den = jnp.sum(p, axis=-1, keepdims=True)
    o = jnp.dot(p.astype(BF16), vh, preferred_element_type=F32) / den
    return o, m + jnp.log(den)


def _scores(qh, kh, bias):
    return lax.dot_general(qh, kh, (((1,), (1,)), ((), ())), preferred_element_type=F32) + bias


def _band_bias(dil):
    slopes = 2.0 ** (-8.0 * (np.arange(N_HEADS, dtype=np.float64) + 1.0) / N_HEADS)
    rel = np.abs((np.arange(K_SUB)[None, :] - HALF) - np.arange(Q_SUB)[:, None]).astype(np.float64)
    bias = -(slopes * dil)[:, None, None] * rel[None]
    return jnp.asarray(np.where(rel[None] <= HALF, bias, NEG_INF), dtype=F32)


def _plane_bias():
    slopes = 2.0 ** (-8.0 * (np.arange(N_HEADS, dtype=np.float64) + 1.0) / N_HEADS)
    qr, qm = np.divmod(np.arange(Q_SUB), Q_SUB // N_PLANES)
    kr, km = np.divmod(np.arange(K_SUB), K_SUB // N_PLANES)
    rel = np.abs(N_PLANES * (km[None, :] - HALF // N_PLANES - qm[:, None]) + (kr[None, :] - qr[:, None])).astype(np.float64)
    bias = -slopes[:, None, None] * rel[None]
    return jnp.asarray(np.where(rel[None] <= HALF, bias, NEG_INF), dtype=F32)


def _attn_class_kernel(q_ref, kp_ref, kc_ref, kn_ref, vp_ref, vc_ref, vn_ref, bias_ref,
                       o_ref, lse_ref, kext, vext, *, lq, n_i):
    i = pl.program_id(1)
    n_sub = lq // Q_SUB
    col = lax.broadcasted_iota(jnp.int32, (Q_SUB, K_SUB), 1)
    lane = lax.broadcasted_iota(jnp.int32, (Q_SUB, LANES), 1)

    def plane(r, carry):
        kext[0:HALF, :] = kp_ref[r]
        kext[HALF:HALF + lq, :] = kc_ref[r]
        kext[HALF + lq:, :] = kn_ref[r]
        vext[0:HALF, :] = vp_ref[r]
        vext[HALF:HALF + lq, :] = vc_ref[r]
        vext[HALF + lq:, :] = vn_ref[r]
        for sb in range(n_sub):
            r0 = sb * Q_SUB
            lse_tile = jnp.zeros((Q_SUB, LANES), F32)
            for h in range(N_HEADS):
                hc = _lane_slab(h)
                s = _scores(q_ref[r, r0:r0 + Q_SUB, hc], kext[r0:r0 + K_SUB, hc], bias_ref[h])
                if sb == 0:
                    s = jnp.where((col >= HALF) | (i > 0), s, NEG_INF)
                if sb == n_sub - 1:
                    s = jnp.where((col < Q_SUB + HALF) | (i < n_i - 1), s, NEG_INF)
                o, lse = _softmax_pv(s, vext[r0:r0 + K_SUB, hc])
                o_ref[r, r0:r0 + Q_SUB, hc] = o.astype(BF16)
                lse_tile = jnp.where(lane == h, lse, lse_tile)
            lse_ref[r, r0:r0 + Q_SUB, :] = lse_tile
        return carry

    lax.fori_loop(0, N_PLANES, plane, 0)


def _attn_d4(z4, B, nseg):
    aw = ATTN_WIDTH
    lq = PLANE_ROWS
    hb = lq // HALF

    def cur(part):
        return pl.BlockSpec((None, None, N_PLANES, lq, aw), lambda b, i: (b, i, 0, 0, part))

    def prev(part):
        return pl.BlockSpec((None, None, N_PLANES, HALF, aw), lambda b, i: (b, jnp.maximum(i - 1, 0), 0, hb - 1, part))

    def nxt(part):
        return pl.BlockSpec((None, None, N_PLANES, HALF, aw), lambda b, i: (b, jnp.minimum(i + 1, nseg - 1), 0, 0, part))

    return pl.pallas_call(
        functools.partial(_attn_class_kernel, lq=lq, n_i=nseg),
        grid=(B, nseg),
        in_specs=[cur(0), prev(1), cur(1), nxt(1), prev(2), cur(2), nxt(2),
                  pl.BlockSpec((N_HEADS, Q_SUB, K_SUB), lambda b, i: (0, 0, 0))],
        out_specs=[
            pl.BlockSpec((None, None, N_PLANES, lq, aw), lambda b, i: (b, i, 0, 0, 0)),
            pl.BlockSpec((None, None, N_PLANES, lq, LANES), lambda b, i: (b, i, 0, 0, 0)),
        ],
        out_shape=[
            jax.ShapeDtypeStruct((B, nseg, N_PLANES, PLANE_ROWS, aw), BF16),
            jax.ShapeDtypeStruct((B, nseg, N_PLANES, PLANE_ROWS, LANES), F32),
        ],
        scratch_shapes=[pltpu.VMEM((lq + 2 * HALF, aw), BF16), pltpu.VMEM((lq + 2 * HALF, aw), BF16)],
        compiler_params=_params(("parallel", "arbitrary"), 52),
        name="attn_d4",
    )(z4, z4, z4, z4, z4, z4, z4, _band_bias(4))


def _attn_d16_kernel(q_ref, kp_ref, kc_ref, kn_ref, vp_ref, vc_ref, vn_ref, bias_ref, o_ref, lse_ref, *, n_i, n_pl):
    i = pl.program_id(2)
    col = lax.broadcasted_iota(jnp.int32, (Q_SUB, K_SUB), 1)
    lane = lax.broadcasted_iota(jnp.int32, (Q_SUB, LANES), 1)

    def plane(r, carry):
        for a in range(4):
            lse_tile = jnp.zeros((Q_SUB, LANES), F32)
            for h in range(N_HEADS):
                hc = _lane_slab(h)
                kh = jnp.concatenate([kp_ref[a, r, :, hc], kc_ref[a, r, :, hc], kn_ref[a, r, :, hc]], axis=0)
                vh = jnp.concatenate([vp_ref[a, r, :, hc], vc_ref[a, r, :, hc], vn_ref[a, r, :, hc]], axis=0)
                s = _scores(q_ref[a, r, :, hc], kh, bias_ref[h])
                s = jnp.where((col >= HALF) | (i > 0), s, NEG_INF)
                s = jnp.where((col < Q_SUB + HALF) | (i < n_i - 1), s, NEG_INF)
                o, lse = _softmax_pv(s, vh)
                o_ref[r, h, pl.ds(a, CLASS_ROWS, stride=4), :] = o
                lse_tile = jnp.where(lane == h, lse, lse_tile)
            lse_ref[r, pl.ds(a, CLASS_ROWS, stride=4), :] = lse_tile
        return carry

    lax.fori_loop(0, n_pl, plane, 0)


def _attn_d16(z16, B, nseg):
    aw = ATTN_WIDTH
    lq = CLASS_ROWS
    assert lq == Q_SUB
    hb = lq // HALF
    n_pl = 2

    def cur(part):
        return pl.BlockSpec((None, None, 4, n_pl, lq, aw), lambda b, r, i: (b, i, 0, r, 0, part))

    def prev(part):
        return pl.BlockSpec((None, None, 4, n_pl, HALF, aw),
                            lambda b, r, i: (b, jnp.maximum(i - 1, 0), 0, r, hb - 1, part))

    def nxt(part):
        return pl.BlockSpec((None, None, 4, n_pl, HALF, aw),
                            lambda b, r, i: (b, jnp.minimum(i + 1, nseg - 1), 0, r, 0, part))

    return pl.pallas_call(
        functools.partial(_attn_d16_kernel, n_i=nseg, n_pl=n_pl),
        grid=(B, N_PLANES // n_pl, nseg),
        in_specs=[cur(0), prev(1), cur(1), nxt(1), prev(2), cur(2), nxt(2),
                  pl.BlockSpec((N_HEADS, Q_SUB, K_SUB), lambda b, r, i: (0, 0, 0))],
        out_specs=[
            pl.BlockSpec((None, None, n_pl, N_HEADS, PLANE_ROWS, LANES), lambda b, r, i: (b, i, r, 0, 0, 0)),
            pl.BlockSpec((None, None, n_pl, PLANE_ROWS, LANES), lambda b, r, i: (b, i, r, 0, 0)),
        ],
        out_shape=[
            jax.ShapeDtypeStruct((B, nseg, N_PLANES, N_HEADS, PLANE_ROWS, LANES), F32),
            jax.ShapeDtypeStruct((B, nseg, N_PLANES, PLANE_ROWS, LANES), F32),
        ],
        compiler_params=_params(("parallel", "parallel", "arbitrary"), 40),
        name="attn_d16",
    )(z16, z16, z16, z16, z16, z16, z16, _band_bias(16))


def _attn_d1_kernel(q_ref, kp_ref, kc_ref, kn_ref, vp_ref, vc_ref, vn_ref, bias_ref,
                    o_ref, lse_ref, kext, vext, *, mq, n_t):
    t = pl.program_id(1)
    qm = Q_SUB // N_PLANES
    km = K_SUB // N_PLANES
    hm = HALF // N_PLANES
    for r in range(N_PLANES):
        kext[r, 0:hm, :] = kp_ref[r]
        kext[r, hm:hm + mq, :] = kc_ref[r]
        kext[r, hm + mq:, :] = kn_ref[r]
        vext[r, 0:hm, :] = vp_ref[r]
        vext[r, hm:hm + mq, :] = vc_ref[r]
        vext[r, hm + mq:, :] = vn_ref[r]

    n_sub = mq // qm
    colm = lax.broadcasted_iota(jnp.int32, (Q_SUB, K_SUB), 1) & (km - 1)
    lane = lax.broadcasted_iota(jnp.int32, (Q_SUB, LANES), 1)
    for sb in range(n_sub):
        m1 = sb * qm
        lse_tile = jnp.zeros((Q_SUB, LANES), F32)
        for h in range(N_HEADS):
            hc = _lane_slab(h)
            qh = jnp.concatenate([q_ref[r, m1:m1 + qm, hc] for r in range(N_PLANES)], axis=0)
            kh = jnp.concatenate([kext[r, m1:m1 + km, hc] for r in range(N_PLANES)], axis=0)
            vh = jnp.concatenate([vext[r, m1:m1 + km, hc] for r in range(N_PLANES)], axis=0)
            s = _scores(qh, kh, bias_ref[h])
            if sb == 0:
                s = jnp.where((colm >= hm) | (t > 0), s, NEG_INF)
            if sb == n_sub - 1:
                s = jnp.where((colm < km - hm) | (t < n_t - 1), s, NEG_INF)
            o, lse = _softmax_pv(s, vh)
            for r in range(N_PLANES):
                o_ref[r, m1:m1 + qm, hc] = o[r * qm:(r + 1) * qm].astype(BF16)
            lse_tile = jnp.where(lane == h, lse, lse_tile)
        for r in range(N_PLANES):
            lse_ref[r, m1:m1 + qm, :] = lse_tile[r * qm:(r + 1) * qm]


def _attn_d1(z4, B, nseg):
    aw = ATTN_WIDTH
    mq = 256
    hm = HALF // N_PLANES
    per_seg = PLANE_ROWS // mq
    hb_step = mq // hm
    hb_seg = PLANE_ROWS // hm
    n_t = nseg * per_seg

    def cur(part):
        return pl.BlockSpec((None, None, N_PLANES, mq, aw), lambda b, t: (b, t // per_seg, 0, t % per_seg, part))

    def prev(part):
        def imap(b, t):
            q = jnp.maximum(t * hb_step - 1, 0)
            return (b, q // hb_seg, 0, q % hb_seg, part)
        return pl.BlockSpec((None, None, N_PLANES, hm, aw), imap)

    def nxt(part):
        def imap(b, t):
            q = jnp.minimum((t + 1) * hb_step, nseg * hb_seg - 1)
            return (b, q // hb_seg, 0, q % hb_seg, part)
        return pl.BlockSpec((None, None, N_PLANES, hm, aw), imap)

    return pl.pallas_call(
        functools.partial(_attn_d1_kernel, mq=mq, n_t=n_t),
        grid=(B, n_t),
        in_specs=[cur(0), prev(1), cur(1), nxt(1), prev(2), cur(2), nxt(2),
                  pl.BlockSpec((N_HEADS, Q_SUB, K_SUB), lambda b, t: (0, 0, 0))],
        out_specs=[
            pl.BlockSpec((None, None, N_PLANES, mq, aw), lambda b, t: (b, t // per_seg, 0, t % per_seg, 0)),
            pl.BlockSpec((None, None, N_PLANES, mq, LANES), lambda b, t: (b, t // per_seg, 0, t % per_seg, 0)),
        ],
        out_shape=[
            jax.ShapeDtypeStruct((B, nseg, N_PLANES, PLANE_ROWS, aw), BF16),
            jax.ShapeDtypeStruct((B, nseg, N_PLANES, PLANE_ROWS, LANES), F32),
        ],
        scratch_shapes=[pltpu.VMEM((N_PLANES, mq + 2 * hm, aw), BF16), pltpu.VMEM((N_PLANES, mq + 2 * hm, aw), BF16)],
        compiler_params=_params(("parallel", "arbitrary"), 32),
        name="attn_d1",
    )(z4, z4, z4, z4, z4, z4, z4, _plane_bias())


def _dft_tables():
    n = SEG
    n1 = 32
    p_per = n1 // N_PLANES
    cols_nat = np.arange(n, dtype=np.float64)
    cols_plane = (N_PLANES * (np.arange(n) % PLANE_ROWS) + np.arange(n) // PLANE_ROWS).astype(np.float64)

    def build(cols):
        a = 2.0 * np.pi * ((np.arange(n // n1)[:, None] * n1 * cols[None, :]) % n) / n
        jr = (N_PLANES * np.arange(p_per)[None, :] + np.arange(N_PLANES)[:, None]).astype(np.float64)
        b = 2.0 * np.pi * ((jr[:, :, None] * cols[None, None, :]) % n) / n
        ca, sa = jnp.asarray(np.cos(a), F32)[None, :, None, :], jnp.asarray(np.sin(a), F32)[None, :, None, :]
        cb, sb = jnp.asarray(np.cos(b), F32)[:, None, :, :], jnp.asarray(np.sin(b), F32)[:, None, :, :]
        cos = (ca * cb - sa * sb).reshape(n, n).astype(BF16)
        msin = (-(sa * cb + ca * sb)).reshape(n, n).astype(BF16)
        return cos, msin

    return build(cols_nat), build(cols_plane)


def _fold_kernel(c_ref, s_ref, w_ref, o_ref):
    w = w_ref[...]
    o_ref[:, :GROUP_DIM] = jnp.dot(c_ref[...], w, preferred_element_type=F32,
                                   precision=lax.Precision.HIGHEST).astype(BF16)
    o_ref[:, GROUP_DIM:] = jnp.dot(s_ref[...], w, preferred_element_type=F32,
                                   precision=lax.Precision.HIGHEST).astype(BF16)


def _fold_channel_dft(w_fourier):
    n = GROUP_DIM
    idx = (np.arange(n)[:, None] * np.arange(n)[None, :]) % n
    ang = 2.0 * np.pi * idx / n
    cos_c = jnp.asarray(np.cos(ang), F32)
    sin_c = jnp.asarray(np.sin(ang), F32)
    return pl.pallas_call(
        _fold_kernel,
        grid=(N_GROUPS,),
        in_specs=[
            pl.BlockSpec((n, n), lambda g: (0, 0)),
            pl.BlockSpec((n, n), lambda g: (0, 0)),
            pl.BlockSpec((None, n, n), lambda g: (g, 0, 0)),
        ],
        out_specs=pl.BlockSpec((None, n, 2 * n), lambda g: (g, 0, 0)),
        out_shape=jax.ShapeDtypeStruct((N_GROUPS, n, 2 * n), BF16),
        compiler_params=_params(("parallel",), 16),
        name="fold_channel_dft",
    )(cos_c, sin_c, w_fourier)


def _seq_dft_kernel(c_ref, ms_ref, u_ref, w_ref, *refs, norm, both):
    n_out = 2 if both else 1
    out_refs, (va_ref, vb_ref) = refs[:n_out], refs[n_out:]
    i = pl.program_id(2)

    @pl.when(i == 0)
    def _():
        for blk in range(SEG // PLANE_ROWS):
            rows = slice(blk * PLANE_ROWS, (blk + 1) * PLANE_ROWS)
            for g in range(N_GROUPS):
                cols = slice(g * GROUP_DIM, (g + 1) * GROUP_DIM)
                v = jnp.dot(u_ref[blk, :, cols], w_ref[g], preferred_element_type=F32)
                va_ref[rows, cols] = v[:, :GROUP_DIM].astype(BF16)
                vb_ref[rows, cols] = v[:, GROUP_DIM:].astype(BF16)

    c = c_ref[...]
    ms = ms_ref[...]
    va = va_ref[...]
    vb = vb_ref[...]
    y1 = jnp.dot(c, va, preferred_element_type=F32) + jnp.dot(ms, vb, preferred_element_type=F32)
    if both:
        y2 = jnp.dot(c, vb, preferred_element_type=F32) - jnp.dot(ms, va, preferred_element_type=F32)
        out_refs[0][...] = y1
        out_refs[1][...] = y2
    else:
        out_refs[0][...] = (y1 * norm).astype(BF16)


def _seq_dft(z4, wcs, tables, B, nseg, norm):
    fw = FOURIER_WIDTH
    tm = 512
    both = nseg > 1
    u_col = IN_COLS // fw - 1
    if both:
        assert nseg == N_PLANES
        n_cls = N_PLANES
        u_spec = pl.BlockSpec((None, nseg, None, PLANE_ROWS, fw), lambda b, s, i: (b, 0, s, 0, u_col))
    else:
        n_cls = 1
        u_spec = pl.BlockSpec((None, None, N_PLANES, PLANE_ROWS, fw), lambda b, s, i: (b, 0, 0, 0, u_col))
    m_spec = pl.BlockSpec((tm, SEG), lambda b, s, i: (i, 0))
    o_spec = pl.BlockSpec((None, None, tm, fw), lambda b, s, i: (b, s, i, 0))
    odt = F32 if both else BF16
    outs = pl.pallas_call(
        functools.partial(_seq_dft_kernel, norm=norm, both=both),
        grid=(B, n_cls, SEG // tm),
        in_specs=[m_spec, m_spec, u_spec,
                  pl.BlockSpec((N_GROUPS, GROUP_DIM, 2 * GROUP_DIM), lambda b, s, i: (0, 0, 0))],
        out_specs=[o_spec] * (2 if both else 1),
        out_shape=[jax.ShapeDtypeStruct((B, n_cls, SEG, fw), odt)] * (2 if both else 1),
        scratch_shapes=[pltpu.VMEM((SEG, fw), BF16), pltpu.VMEM((SEG, fw), BF16)],
        compiler_params=_params(("parallel", "parallel", "arbitrary"), 48),
        name="four_seq_r4" if both else "four_seq_r1",
    )(tables[0], tables[1], z4, wcs)
    return outs if both else outs[0]


def _twiddle_kernel(y1_ref, y2_ref, tw_ref, f_ref, *, radix, norm):
    tw = tw_ref[...]
    acc = None
    for s in range(radix):
        term = tw[:, s:s + 1] * y1_ref[s] - tw[:, radix + s:radix + s + 1] * y2_ref[s]
        acc = term if acc is None else acc + term
    f_ref[...] = (acc * norm).astype(BF16)


def _twiddle_combine(y1, y2, norm):
    B, radix, _, fw = y1.shape
    nseg = radix
    S = nseg * SEG
    tm = 256
    g, r, m = np.meshgrid(np.arange(nseg), np.arange(N_PLANES), np.arange(PLANE_ROWS), indexing="ij")
    j = (SEG * g + N_PLANES * m + r).astype(np.float64)[..., None]
    ang = 2.0 * np.pi * ((j * np.arange(radix)) % S) / S
    tw = np.zeros((nseg, N_PLANES, PLANE_ROWS, LANES), np.float32)
    tw[..., :radix] = np.cos(ang)
    tw[..., radix:2 * radix] = np.sin(ang)
    nb = PLANE_ROWS // tm
    y_spec = pl.BlockSpec((None, radix, tm, fw), lambda b, r, i, g: (b, 0, r * nb + i, 0))
    return pl.pallas_call(
        functools.partial(_twiddle_kernel, radix=radix, norm=norm),
        grid=(B, N_PLANES, nb, nseg),
        in_specs=[y_spec, y_spec, pl.BlockSpec((None, None, tm, LANES), lambda b, r, i, g: (g, r, i, 0))],
        out_specs=pl.BlockSpec((None, None, None, tm, fw), lambda b, r, i, g: (b, g, r, i, 0)),
        out_shape=jax.ShapeDtypeStruct((B, nseg, N_PLANES, PLANE_ROWS, fw), BF16),
        compiler_params=_params(("parallel", "parallel", "parallel", "arbitrary"), 48),
        name="four_twiddle",
    )(y1, y2, jnp.asarray(tw))


def _fourier(z4s, wcs, tables_nat, tables_plane, B, nseg):
    G = B * nseg
    fw = FOURIER_WIDTH
    norm = 1.0 / math.sqrt(nseg * SEG * GROUP_DIM)
    if nseg == 1:
        f = _seq_dft(z4s, wcs, tables_plane, B, nseg, norm)
    else:
        y1, y2 = _seq_dft(z4s, wcs, tables_nat, B, nseg, norm)
        f = _twiddle_combine(y1, y2, norm)
    return f.reshape(G, N_PLANES, PLANE_ROWS, fw)


def _out_proj_kernel(o1_ref, o4_ref, o16_ref, l1_ref, l4_ref, l16_ref, f_ref, x_ref, mod_ref,
                     ga_ref, gf_ref, gpm_ref, gpl_ref, e2_ref, w_ref, x1_ref, h2_ref,
                     xslab, wcat, wbc, mixed, ybuf, *, tm):
    aw = ATTN_WIDTH
    pr = tm // N_PLANES
    for s in range(D_SLABS):
        xslab[s] = x_ref[:, _lane_slab(s)]

    for r in range(N_PLANES):
        rows = slice(r * pr, (r + 1) * pr)
        l1, l4, l16 = l1_ref[r], l4_ref[r], l16_ref[r]
        top = jnp.maximum(jnp.maximum(l1, l4), l16)
        e1, e4, e16 = jnp.exp(l1 - top), jnp.exp(l4 - top), jnp.exp(l16 - top)
        inv = 1.0 / (e1 + e4 + e16)
        for b, e in enumerate((e1, e4, e16)):
            w = e * inv
            hi = w.astype(BF16)
            wcat[b, rows, 0:LANES] = hi
            wcat[b, rows, LANES:] = (w - hi.astype(F32)).astype(BF16)
    for b in range(3):
        wbc[b] = jnp.dot(wcat[b], e2_ref[...], preferred_element_type=F32)

    for r in range(N_PLANES):
        rows = slice(r * pr, (r + 1) * pr)
        o16 = jnp.concatenate([o16_ref[r, h] for h in range(N_HEADS)], axis=1)
        a = (wbc[0, rows, :] * o1_ref[r].astype(F32) + wbc[1, rows, :] * o4_ref[r].astype(F32)
             + wbc[2, rows, :] * o16)
        mixed[rows, :aw] = _rms_rows(a, ga_ref[...]).astype(BF16)
        mixed[rows, aw:] = _rms_rows(f_ref[r].astype(F32), gf_ref[...]).astype(BF16)

    ybuf[...] = jnp.dot(mixed[...], w_ref[...], preferred_element_type=F32)

    gate1 = mod_ref[2:3, :]
    shift2 = mod_ref[3:4, :]
    scale2 = 1.0 + mod_ref[4:5, :]
    for r in range(N_PLANES):
        rows = slice(r * pr, (r + 1) * pr)
        xr = jnp.concatenate([xslab[s, pl.ds(r, pr, stride=N_PLANES), :] for s in range(D_SLABS)], axis=1)
        x1 = xr + gate1 * _rms_rows(ybuf[rows, :], gpm_ref[...])
        x1_ref[r] = x1
        h2_ref[r] = (_rms_rows(x1, gpl_ref[...]) * scale2 + shift2).astype(BF16)


def _lane_broadcast_matrix():
    e = np.zeros((2 * LANES, ATTN_WIDTH), np.float32)
    for h in range(N_HEADS):
        e[h, h * HEAD_DIM:(h + 1) * HEAD_DIM] = 1.0
        e[LANES + h, h * HEAD_DIM:(h + 1) * HEAD_DIM] = 1.0
    return jnp.asarray(e, BF16)


def _out_proj(o1, o4, o16, l1, l4, l16, f, xg, mod, nseg, g_attn_out, g_fourier_out, g_post_mix, g_pre_mlp, w_out_bf16):
    G = xg.shape[0]
    tm = 256
    pr = tm // N_PLANES
    aw, fw = ATTN_WIDTH, FOURIER_WIDTH
    plane = lambda g, i: (g, 0, i, 0)
    const2 = lambda g, i: (0, 0)
    return pl.pallas_call(
        functools.partial(_out_proj_kernel, tm=tm),
        grid=(G, SEG // tm),
        in_specs=[
            pl.BlockSpec((None, N_PLANES, pr, aw), plane),
            pl.BlockSpec((None, N_PLANES, pr, aw), plane),
            pl.BlockSpec((None, N_PLANES, N_HEADS, pr, LANES), lambda g, i: (g, 0, 0, i, 0)),
            pl.BlockSpec((None, N_PLANES, pr, LANES), plane),
            pl.BlockSpec((None, N_PLANES, pr, LANES), plane),
            pl.BlockSpec((None, N_PLANES, pr, LANES), plane),
            pl.BlockSpec((None, N_PLANES, pr, fw), plane),
            pl.BlockSpec((None, tm, D_MODEL), lambda g, i: (g, i, 0)),
            pl.BlockSpec((None, N_MOD, D_MODEL), lambda g, i: (g // nseg, 0, 0)),
            pl.BlockSpec((1, aw), const2),
            pl.BlockSpec((1, fw), const2),
            pl.BlockSpec((1, D_MODEL), const2),
            pl.BlockSpec((1, D_MODEL), const2),
            pl.BlockSpec((2 * LANES, aw), const2),
            pl.BlockSpec((aw + fw, D_MODEL), const2),
        ],
        out_specs=[pl.BlockSpec((None, N_PLANES, pr, D_MODEL), plane)] * 2,
        out_shape=[jax.ShapeDtypeStruct((G, N_PLANES, PLANE_ROWS, D_MODEL), F32),
                   jax.ShapeDtypeStruct((G, N_PLANES, PLANE_ROWS, D_MODEL), BF16)],
        scratch_shapes=[
            pltpu.VMEM((D_SLABS, tm, LANES), F32),
            pltpu.VMEM((3, tm, 2 * LANES), BF16),
            pltpu.VMEM((3, tm, aw), F32),
            pltpu.VMEM((tm, aw + fw), BF16),
            pltpu.VMEM((tm, D_MODEL), F32),
        ],
        compiler_params=_params(("parallel", "parallel"), 56),
        name="out_proj",
    )(o1, o4, o16, l1, l4, l16, f, xg, mod, g_attn_out.reshape(1, aw), g_fourier_out.reshape(1, fw),
      g_post_mix.reshape(1, D_MODEL), g_pre_mlp.reshape(1, D_MODEL), _lane_broadcast_matrix(), w_out_bf16)


def _mlp_kernel(h_ref, x1_ref, mod_ref, g_ref, w1_hbm, w2_hbm, out_ref, w1buf, w2buf, sem, acc_ref, oslab,
                *, tm, tf, n_c, n_steps):
    step = pl.program_id(0) * pl.num_programs(1) + pl.program_id(1)
    pr = tm // N_PLANES

    def weight_copies(c, slot):
        c0 = pl.multiple_of(c * tf, tf)
        return (
            pltpu.make_async_copy(w1_hbm.at[:, pl.ds(c0, tf)], w1buf.at[slot], sem.at[0, slot]),
            pltpu.make_async_copy(w2_hbm.at[pl.ds(c0, tf), :], w2buf.at[slot], sem.at[1, slot]),
        )

    @pl.when(step == 0)
    def _():
        for cp in weight_copies(0, 0):
            cp.start()

    acc_ref[...] = jnp.zeros_like(acc_ref)

    def chunk_pair(it, carry):
        for slot in range(2):
            c = 2 * it + slot
            for cp in weight_copies(c, slot):
                cp.wait()
            if slot == 0:
                for cp in weight_copies(c + 1, 1):
                    cp.start()
            else:
                more = c + 1 < n_c

                @pl.when(more | (step < n_steps - 1))
                def _():
                    for cp in weight_copies(jnp.where(more, c + 1, 0), 0):
                        cp.start()

            hid = jnp.dot(h_ref[...].reshape(tm, D_MODEL), w1buf[slot], preferred_element_type=F32)
            hid = jnp.square(jnp.maximum(hid, 0.0)).astype(BF16)
            acc_ref[...] += jnp.dot(hid, w2buf[slot], preferred_element_type=F32)
        return carry

    lax.fori_loop(0, n_c // 2, chunk_pair, 0)

    gate2 = mod_ref[5:6, :]
    rc = 64
    for r in range(N_PLANES):
        def body(c, carry):
            m0 = pl.multiple_of(c * rc, rc)
            y = acc_ref[pl.ds(r * pr + m0, rc), :]
            v = x1_ref[r, pl.ds(m0, rc), :] + gate2 * _rms_rows(y, g_ref[...])
            for s in range(D_SLABS):
                oslab[s, pl.ds(r + N_PLANES * m0, rc, stride=N_PLANES), :] = v[:, _lane_slab(s)]
            return carry

        lax.fori_loop(0, pr // rc, body, 0)
    for s in range(D_SLABS):
        out_ref[:, _lane_slab(s)] = oslab[s]


def _mlp(h2, x1, mod, nseg, g_post_mlp, w1_bf16, w2_bf16):
    G = x1.shape[0]
    tm, tf = 512, 1024
    pr = tm // N_PLANES
    n_c = D_FF // tf
    assert n_c % 2 == 0
    n_i = SEG // tm
    plane = lambda g, i: (g, 0, i, 0)
    return pl.pallas_call(
        functools.partial(_mlp_kernel, tm=tm, tf=tf, n_c=n_c, n_steps=G * n_i),
        grid=(G, n_i),
        in_specs=[
            pl.BlockSpec((None, N_PLANES, pr, D_MODEL), plane),
            pl.BlockSpec((None, N_PLANES, pr, D_MODEL), plane),
            pl.BlockSpec((None, N_MOD, D_MODEL), lambda g, i: (g // nseg, 0, 0)),
            pl.BlockSpec((1, D_MODEL), lambda g, i: (0, 0)),
            pl.BlockSpec(memory_space=pl.ANY),
            pl.BlockSpec(memory_space=pl.ANY),
        ],
        out_specs=pl.BlockSpec((None, tm, D_MODEL), lambda g, i: (g, i, 0)),
        out_shape=jax.ShapeDtypeStruct((G, SEG, D_MODEL), F32),
        scratch_shapes=[
            pltpu.VMEM((2, D_MODEL, tf), BF16),
            pltpu.VMEM((2, tf, D_MODEL), BF16),
            pltpu.SemaphoreType.DMA((2, 2)),
            pltpu.VMEM((tm, D_MODEL), F32),
            pltpu.VMEM((D_SLABS, tm, LANES), F32),
        ],
        compiler_params=_params(("arbitrary", "arbitrary"), 56),
        name="mlp",
    )(h2, x1, mod, g_post_mlp.reshape(1, D_MODEL), w1_bf16, w2_bf16)


def _layer(x, mod, p):
    B, S, _ = x.shape
    assert S % SEG == 0
    nseg = S // SEG
    G = B * nseg
    xg = x.reshape(G, SEG, D_MODEL)
    z4, z16 = _in_proj(xg, mod, nseg, p["g_pre_mix"], p["w_in"])
    z4s = z4.reshape(B, nseg, N_PLANES, PLANE_ROWS, IN_COLS)
    z16s = z16.reshape(B, nseg, 4, N_PLANES, CLASS_ROWS, 3 * ATTN_WIDTH)
    o1, l1 = _attn_d1(z4s, B, nseg)
    o4, l4 = _attn_d4(z4s, B, nseg)
    o16, l16 = _attn_d16(z16s, B, nseg)
    f = _fourier(z4s, p["wcs"], p["tables_nat"], p["tables_plane"], B, nseg)
    pshape = (G, N_PLANES, PLANE_ROWS)
    x1, h2 = _out_proj(
        o1.reshape(pshape + (ATTN_WIDTH,)), o4.reshape(pshape + (ATTN_WIDTH,)),
        o16.reshape((G, N_PLANES, N_HEADS, PLANE_ROWS, LANES)),
        l1.reshape(pshape + (LANES,)), l4.reshape(pshape + (LANES,)), l16.reshape(pshape + (LANES,)),
        f, xg, mod, nseg, p["g_attn_out"], p["g_fourier_out"], p["g_post_mix"], p["g_pre_mlp"], p["w_out"])
    out = _mlp(h2, x1, mod, nseg, p["g_post_mlp"], p["w_mlp_in"], p["w_mlp_out"])
    return out.reshape(B, S, D_MODEL)


def kernel(x_prompt, x_sample, c_prompt, c_sample, w_ada, b_ada, g_pre_mix, w_in, g_attn_out, w_fourier,
           g_fourier_out, w_out, g_post_mix, g_pre_mlp, w_mlp_in, w_mlp_out, g_post_mlp):
    depth = w_ada.shape[0]
    nb_p, nb_s = c_prompt.shape[0], c_sample.shape[0]
    pad = (-(nb_p + nb_s)) % 8
    c_all = jnp.concatenate([c_prompt, c_sample, jnp.zeros((pad, D_MODEL), F32)], axis=0)
    tables_nat, tables_plane = _dft_tables()
    xp, xs = x_prompt, x_sample
    for l in range(depth):
        mod = _modulation(c_all, w_ada[l], b_ada[l])
        mod_p = mod[:nb_p].reshape(nb_p, N_MOD, D_MODEL)
        mod_s = mod[nb_p:nb_p + nb_s].reshape(nb_s, N_MOD, D_MODEL)
        p = dict(
            g_pre_mix=g_pre_mix[l], w_in=w_in[l].astype(BF16), g_attn_out=g_attn_out[l],
            g_fourier_out=g_fourier_out[l], w_out=w_out[l].astype(BF16), g_post_mix=g_post_mix[l],
            g_pre_mlp=g_pre_mlp[l], w_mlp_in=w_mlp_in[l].astype(BF16), w_mlp_out=w_mlp_out[l].astype(BF16),
            g_post_mlp=g_post_mlp[l], wcs=_fold_channel_dft(w_fourier[l]),
            tables_nat=tables_nat, tables_plane=tables_plane,
        )
        xp = _layer(xp, mod_p, p)
        xs = _layer(xs, mod_s, p)
    return (xp, xs)
```

```python
import functools
import math

import numpy as np
import jax
import jax.numpy as jnp
from jax import lax
from jax.experimental import pallas as pl
from jax.experimental.pallas import tpu as pltpu

D_MODEL = 2048
ATTN_WIDTH = 1024
FOURIER_WIDTH = 1024
HEAD_DIM = 128
N_HEADS = 8
N_GROUPS = 4
GROUP_DIM = 256
D_FF = 8192
IN_COLS = 3 * ATTN_WIDTH + FOURIER_WIDTH
DILATED_BRANCHES = ((128, 1), (512, 4), (2048, 16))
HALF = 64
N_MOD = 6
RMS_EPS = 1e-6
NEG_INF = -1e30
LOG2E = math.log2(math.e)
LN2 = math.log(2.0)

SEG = 2048
N_PLANES = 4
PLANE_ROWS = SEG // N_PLANES
N_CLASSES = 16
CLASS_ROWS = SEG // N_CLASSES
LANES = 128
D_SLABS = D_MODEL // LANES
Q_SUB = 128
K_SUB = Q_SUB + 2 * HALF

F32 = jnp.float32
BF16 = jnp.bfloat16
MIB = 1024 * 1024

assert DILATED_BRANCHES == ((128, 1), (512, 4), (2048, 16))
assert all(w // (2 * d) == HALF for w, d in DILATED_BRANCHES)


def _params(semantics, vmem_mib):
    return pltpu.CompilerParams(dimension_semantics=semantics, vmem_limit_bytes=vmem_mib * MIB)


def _rms_rows(v, gain):
    ms = jnp.mean(v * v, axis=-1, keepdims=True)
    return v * lax.rsqrt(ms + RMS_EPS) * gain


def _lane_slab(s):
    return slice(s * LANES, (s + 1) * LANES)


def _mod_kernel(c_ref, w_ref, b_ref, o_ref):
    c = c_ref[...]
    a = (c * jax.nn.sigmoid(c)).astype(BF16)
    o_ref[...] = jnp.dot(a, w_ref[...].astype(BF16), preferred_element_type=F32) + b_ref[...]


def _modulation(c_all, w_ada, b_ada):
    rows = c_all.shape[0]
    n = w_ada.shape[1]
    tn = 1024
    return pl.pallas_call(
        _mod_kernel,
        grid=(n // tn,),
        in_specs=[
            pl.BlockSpec((rows, D_MODEL), lambda j: (0, 0)),
            pl.BlockSpec((D_MODEL, tn), lambda j: (0, j)),
            pl.BlockSpec((1, tn), lambda j: (0, j)),
        ],
        out_specs=pl.BlockSpec((rows, tn), lambda j: (0, j)),
        out_shape=jax.ShapeDtypeStruct((rows, n), F32),
        compiler_params=_params(("parallel",), 40),
        name="mod",
    )(c_all, w_ada, b_ada.reshape(1, n))


def _in_proj_kernel(x_ref, mod_ref, g_ref, w_ref, z4_ref, z16_ref, h_ref, hslab, aslab, *, tm, tn, nb):
    j = pl.program_id(2)
    pr = tm // N_PLANES
    pb = nb // N_PLANES

    @pl.when(j == 0)
    def _():
        shift = mod_ref[0:1, :]
        scale1 = 1.0 + mod_ref[1:2, :]
        gain = g_ref[...]

        def block(c, carry):
            n0 = pl.multiple_of(c * nb, nb)
            rq = 16
            for q in range(nb // rq):
                h = _rms_rows(x_ref[pl.ds(n0 + q * rq, rq), :], gain) * scale1 + shift
                for s in range(D_SLABS):
                    hslab[s, q * rq:(q + 1) * rq, :] = h[:, _lane_slab(s)]
            m0 = pl.multiple_of(c * pb, pb)
            for r in range(N_PLANES):
                for s in range(D_SLABS):
                    h_ref[pl.ds(r * pr + m0, pb), _lane_slab(s)] = (
                        hslab[s, pl.ds(r, pb, stride=N_PLANES), :].astype(BF16))
            return carry

        lax.fori_loop(0, tm // nb, block, 0)

    acc = jnp.dot(h_ref[...], w_ref[...], preferred_element_type=F32)
    acc = acc * jnp.where(j == 1, LOG2E * HEAD_DIM ** -0.5, 1.0).astype(F32)
    for r in range(N_PLANES):
        z4_ref[r] = acc[r * pr:(r + 1) * pr].astype(BF16)
    for s in range(tn // LANES):
        aslab[s] = acc[:, _lane_slab(s)]
    cr = pr // 4
    for r in range(N_PLANES):
        for a in range(4):
            for s in range(tn // LANES):
                z16_ref[r + 4 * a, :, _lane_slab(s)] = (
                    aslab[s, pl.ds(r * pr + a, cr, stride=4), :].astype(BF16))


def _in_proj(xg, mod, nseg, g_pre_mix, w_in_blocks):
    G = xg.shape[0]
    tm = 1024
    n_j, _, tn = w_in_blocks.shape
    return pl.pallas_call(
        functools.partial(_in_proj_kernel, tm=tm, tn=tn, nb=256),
        grid=(G, SEG // tm, n_j),
        in_specs=[
            pl.BlockSpec((None, tm, D_MODEL), lambda g, i, j: (g, i, 0)),
            pl.BlockSpec((None, N_MOD, D_MODEL), lambda g, i, j: (g // nseg, 0, 0)),
            pl.BlockSpec((1, D_MODEL), lambda g, i, j: (0, 0)),
            pl.BlockSpec((None, D_MODEL, tn), lambda g, i, j: ((j + n_j - 1) % n_j, 0, 0)),
        ],
        out_specs=[
            pl.BlockSpec((None, N_PLANES, tm // N_PLANES, tn), lambda g, i, j: (g, 0, i, (j + n_j - 1) % n_j)),
            pl.BlockSpec((None, N_CLASSES, tm // N_CLASSES, tn), lambda g, i, j: (g, 0, i, jnp.maximum(j, 1) - 1)),
        ],
        out_shape=[
            jax.ShapeDtypeStruct((G, N_PLANES, PLANE_ROWS, IN_COLS), BF16),
            jax.ShapeDtypeStruct((G, N_CLASSES, CLASS_ROWS, 3 * ATTN_WIDTH), BF16),
        ],
        scratch_shapes=[
            pltpu.VMEM((tm, D_MODEL), BF16),
            pltpu.VMEM((D_SLABS, 256, LANES), F32),
            pltpu.VMEM((tn // LANES, tm, LANES), F32),
        ],
        compiler_params=_params(("parallel", "parallel", "arbitrary"), 56),
        name="in_proj",
    )(xg, mod, g_pre_mix.reshape(1, D_MODEL), w_in_blocks)


def _softmax_pv(s, vh):
    m = jnp.max(s, axis=-1, keepdims=True)
    p = jnp.exp2(s - m)
    den = jnp.sum(p, axis=-1, keepdims=True)
    o = jnp.dot(p.astype(BF16), vh, preferred_element_type=F32) / den
    return o, (m + jnp.log2(den)) * LN2


def _scores(qh, kh, bias):
    return lax.dot_general(qh, kh, (((1,), (1,)), ((), ())), preferred_element_type=F32) + bias


def _alibi_tile(rel, dil):
    slopes = 2.0 ** (-8.0 * (np.arange(N_HEADS, dtype=np.float64) + 1.0) / N_HEADS)
    bias = -(slopes * dil * LOG2E)[:, None, None] * rel[None]
    return jnp.asarray(np.where(rel[None] <= HALF, bias, NEG_INF), dtype=F32)


def _band_bias(dil):
    rel = np.abs((np.arange(K_SUB)[None, :] - HALF) - np.arange(Q_SUB)[:, None]).astype(np.float64)
    return _alibi_tile(rel, dil)


def _plane_bias():
    qr, qm = np.divmod(np.arange(Q_SUB), Q_SUB // N_PLANES)
    kr, km = np.divmod(np.arange(K_SUB), K_SUB // N_PLANES)
    rel = np.abs(N_PLANES * (km[None, :] - HALF // N_PLANES - qm[:, None]) + (kr[None, :] - qr[:, None])).astype(np.float64)
    return _alibi_tile(rel, 1)


def _attn_class_kernel(q_ref, kp_ref, kc_ref, kn_ref, vp_ref, vc_ref, vn_ref, bias_ref,
                       o_ref, lse_ref, kext, vext, *, lq, n_i):
    i = pl.program_id(1)
    n_sub = lq // Q_SUB
    col = lax.broadcasted_iota(jnp.int32, (Q_SUB, K_SUB), 1)
    lane = lax.broadcasted_iota(jnp.int32, (Q_SUB, LANES), 1)

    def plane(r, carry):
        kext[0:HALF, :] = kp_ref[r]
        kext[HALF:HALF + lq, :] = kc_ref[r]
        kext[HALF + lq:, :] = kn_ref[r]
        vext[0:HALF, :] = vp_ref[r]
        vext[HALF:HALF + lq, :] = vc_ref[r]
        vext[HALF + lq:, :] = vn_ref[r]
        for sb in range(n_sub):
            r0 = sb * Q_SUB
            lse_tile = jnp.zeros((Q_SUB, LANES), F32)
            for h in range(N_HEADS):
                hc = _lane_slab(h)
                s = _scores(q_ref[r, r0:r0 + Q_SUB, hc], kext[r0:r0 + K_SUB, hc], bias_ref[h])
                if sb == 0:
                    s = jnp.where((col >= HALF) | (i > 0), s, NEG_INF)
                if sb == n_sub - 1:
                    s = jnp.where((col < Q_SUB + HALF) | (i < n_i - 1), s, NEG_INF)
                o, lse = _softmax_pv(s, vext[r0:r0 + K_SUB, hc])
                o_ref[r, r0:r0 + Q_SUB, hc] = o.astype(BF16)
                lse_tile = jnp.where(lane == h, lse, lse_tile)
            lse_ref[r, r0:r0 + Q_SUB, :] = lse_tile
        return carry

    lax.fori_loop(0, N_PLANES, plane, 0)


def _attn_d4(z4, B, nseg):
    aw = ATTN_WIDTH
    lq = PLANE_ROWS
    hb = lq // HALF

    def cur(part):
        return pl.BlockSpec((None, None, N_PLANES, lq, aw), lambda b, i: (b, i, 0, 0, part))

    def prev(part):
        return pl.BlockSpec((None, None, N_PLANES, HALF, aw), lambda b, i: (b, jnp.maximum(i - 1, 0), 0, hb - 1, part))

    def nxt(part):
        return pl.BlockSpec((None, None, N_PLANES, HALF, aw), lambda b, i: (b, jnp.minimum(i + 1, nseg - 1), 0, 0, part))

    return pl.pallas_call(
        functools.partial(_attn_class_kernel, lq=lq, n_i=nseg),
        grid=(B, nseg),
        in_specs=[cur(0), prev(1), cur(1), nxt(1), prev(2), cur(2), nxt(2),
                  pl.BlockSpec((N_HEADS, Q_SUB, K_SUB), lambda b, i: (0, 0, 0))],
        out_specs=[
            pl.BlockSpec((None, None, N_PLANES, lq, aw), lambda b, i: (b, i, 0, 0, 0)),
            pl.BlockSpec((None, None, N_PLANES, lq, LANES), lambda b, i: (b, i, 0, 0, 0)),
        ],
        out_shape=[
            jax.ShapeDtypeStruct((B, nseg, N_PLANES, PLANE_ROWS, aw), BF16),
            jax.ShapeDtypeStruct((B, nseg, N_PLANES, PLANE_ROWS, LANES), F32),
        ],
        scratch_shapes=[pltpu.VMEM((lq + 2 * HALF, aw), BF16), pltpu.VMEM((lq + 2 * HALF, aw), BF16)],
        compiler_params=_params(("parallel", "arbitrary"), 52),
        name="attn_d4",
    )(z4, z4, z4, z4, z4, z4, z4, _band_bias(4))


def _attn_d16_kernel(q_ref, kp_ref, kc_ref, kn_ref, vp_ref, vc_ref, vn_ref, bias_ref, o_ref, lse_ref, *, n_i, n_pl):
    i = pl.program_id(2)
    col = lax.broadcasted_iota(jnp.int32, (Q_SUB, K_SUB), 1)
    lane = lax.broadcasted_iota(jnp.int32, (Q_SUB, LANES), 1)

    def plane(r, carry):
        for a in range(4):
            lse_tile = jnp.zeros((Q_SUB, LANES), F32)
            for h in range(N_HEADS):
                hc = _lane_slab(h)
                kh = jnp.concatenate([kp_ref[a, r, :, hc], kc_ref[a, r, :, hc], kn_ref[a, r, :, hc]], axis=0)
                vh = jnp.concatenate([vp_ref[a, r, :, hc], vc_ref[a, r, :, hc], vn_ref[a, r, :, hc]], axis=0)
                s = _scores(q_ref[a, r, :, hc], kh, bias_ref[h])
                s = jnp.where((col >= HALF) | (i > 0), s, NEG_INF)
                s = jnp.where((col < Q_SUB + HALF) | (i < n_i - 1), s, NEG_INF)
                o, lse = _softmax_pv(s, vh)
                o_ref[r, h, pl.ds(a, CLASS_ROWS, stride=4), :] = o
                lse_tile = jnp.where(lane == h, lse, lse_tile)
            lse_ref[r, pl.ds(a, CLASS_ROWS, stride=4), :] = lse_tile
        return carry

    lax.fori_loop(0, n_pl, plane, 0)


def _attn_d16(z16, B, nseg):
    aw = ATTN_WIDTH
    lq = CLASS_ROWS
    assert lq == Q_SUB
    hb = lq // HALF
    n_pl = 2

    def cur(part):
        return pl.BlockSpec((None, None, 4, n_pl, lq, aw), lambda b, r, i: (b, i, 0, r, 0, part))

    def prev(part):
        return pl.BlockSpec((None, None, 4, n_pl, HALF, aw),
                            lambda b, r, i: (b, jnp.maximum(i - 1, 0), 0, r, hb - 1, part))

    def nxt(part):
        return pl.BlockSpec((None, None, 4, n_pl, HALF, aw),
                            lambda b, r, i: (b, jnp.minimum(i + 1, nseg - 1), 0, r, 0, part))

    return pl.pallas_call(
        functools.partial(_attn_d16_kernel, n_i=nseg, n_pl=n_pl),
        grid=(B, N_PLANES // n_pl, nseg),
        in_specs=[cur(0), prev(1), cur(1), nxt(1), prev(2), cur(2), nxt(2),
                  pl.BlockSpec((N_HEADS, Q_SUB, K_SUB), lambda b, r, i: (0, 0, 0))],
        out_specs=[
            pl.BlockSpec((None, None, n_pl, N_HEADS, PLANE_ROWS, LANES), lambda b, r, i: (b, i, r, 0, 0, 0)),
            pl.BlockSpec((None, None, n_pl, PLANE_ROWS, LANES), lambda b, r, i: (b, i, r, 0, 0)),
        ],
        out_shape=[
            jax.ShapeDtypeStruct((B, nseg, N_PLANES, N_HEADS, PLANE_ROWS, LANES), F32),
            jax.ShapeDtypeStruct((B, nseg, N_PLANES, PLANE_ROWS, LANES), F32),
        ],
        compiler_params=_params(("parallel", "parallel", "arbitrary"), 40),
        name="attn_d16",
    )(z16, z16, z16, z16, z16, z16, z16, _band_bias(16))


def _attn_d1_kernel(q_ref, kp_ref, kc_ref, kn_ref, vp_ref, vc_ref, vn_ref, bias_ref,
                    o_ref, lse_ref, kext, vext, *, mq, n_t):
    t = pl.program_id(1)
    qm = Q_SUB // N_PLANES
    km = K_SUB // N_PLANES
    hm = HALF // N_PLANES
    for r in range(N_PLANES):
        kext[r, 0:hm, :] = kp_ref[r]
        kext[r, hm:hm + mq, :] = kc_ref[r]
        kext[r, hm + mq:, :] = kn_ref[r]
        vext[r, 0:hm, :] = vp_ref[r]
        vext[r, hm:hm + mq, :] = vc_ref[r]
        vext[r, hm + mq:, :] = vn_ref[r]

    n_sub = mq // qm
    colm = lax.broadcasted_iota(jnp.int32, (Q_SUB, K_SUB), 1) & (km - 1)
    lane = lax.broadcasted_iota(jnp.int32, (Q_SUB, LANES), 1)
    for sb in range(n_sub):
        m1 = sb * qm
        lse_tile = jnp.zeros((Q_SUB, LANES), F32)
        for h in range(N_HEADS):
            hc = _lane_slab(h)
            qh = jnp.concatenate([q_ref[r, m1:m1 + qm, hc] for r in range(N_PLANES)], axis=0)
            kh = jnp.concatenate([kext[r, m1:m1 + km, hc] for r in range(N_PLANES)], axis=0)
            vh = jnp.concatenate([vext[r, m1:m1 + km, hc] for r in range(N_PLANES)], axis=0)
            s = _scores(qh, kh, bias_ref[h])
            if sb == 0:
                s = jnp.where((colm >= hm) | (t > 0), s, NEG_INF)
            if sb == n_sub - 1:
                s = jnp.where((colm < km - hm) | (t < n_t - 1), s, NEG_INF)
            o, lse = _softmax_pv(s, vh)
            for r in range(N_PLANES):
                o_ref[r, m1:m1 + qm, hc] = o[r * qm:(r + 1) * qm].astype(BF16)
            lse_tile = jnp.where(lane == h, lse, lse_tile)
        for r in range(N_PLANES):
            lse_ref[r, m1:m1 + qm, :] = lse_tile[r * qm:(r + 1) * qm]


def _attn_d1(z4, B, nseg):
    aw = ATTN_WIDTH
    mq = 256
    hm = HALF // N_PLANES
    per_seg = PLANE_ROWS // mq
    hb_step = mq // hm
    hb_seg = PLANE_ROWS // hm
    n_t = nseg * per_seg

    def cur(part):
        return pl.BlockSpec((None, None, N_PLANES, mq, aw), lambda b, t: (b, t // per_seg, 0, t % per_seg, part))

    def prev(part):
        def imap(b, t):
            q = jnp.maximum(t * hb_step - 1, 0)
            return (b, q // hb_seg, 0, q % hb_seg, part)
        return pl.BlockSpec((None, None, N_PLANES, hm, aw), imap)

    def nxt(part):
        def imap(b, t):
            q = jnp.minimum((t + 1) * hb_step, nseg * hb_seg - 1)
            return (b, q // hb_seg, 0, q % hb_seg, part)
        return pl.BlockSpec((None, None, N_PLANES, hm, aw), imap)

    return pl.pallas_call(
        functools.partial(_attn_d1_kernel, mq=mq, n_t=n_t),
        grid=(B, n_t),
        in_specs=[cur(0), prev(1), cur(1), nxt(1), prev(2), cur(2), nxt(2),
                  pl.BlockSpec((N_HEADS, Q_SUB, K_SUB), lambda b, t: (0, 0, 0))],
        out_specs=[
            pl.BlockSpec((None, None, N_PLANES, mq, aw), lambda b, t: (b, t // per_seg, 0, t % per_seg, 0)),
            pl.BlockSpec((None, None, N_PLANES, mq, LANES), lambda b, t: (b, t // per_seg, 0, t % per_seg, 0)),
        ],
        out_shape=[
            jax.ShapeDtypeStruct((B, nseg, N_PLANES, PLANE_ROWS, aw), BF16),
            jax.ShapeDtypeStruct((B, nseg, N_PLANES, PLANE_ROWS, LANES), F32),
        ],
        scratch_shapes=[pltpu.VMEM((N_PLANES, mq + 2 * hm, aw), BF16), pltpu.VMEM((N_PLANES, mq + 2 * hm, aw), BF16)],
        compiler_params=_params(("parallel", "arbitrary"), 32),
        name="attn_d1",
    )(z4, z4, z4, z4, z4, z4, z4, _plane_bias())


def _dft_tables():
    n = SEG
    n1 = 32
    p_per = n1 // N_PLANES
    cols_nat = np.arange(n, dtype=np.float64)
    cols_plane = (N_PLANES * (np.arange(n) % PLANE_ROWS) + np.arange(n) // PLANE_ROWS).astype(np.float64)

    def build(cols):
        a = 2.0 * np.pi * ((np.arange(n // n1)[:, None] * n1 * cols[None, :]) % n) / n
        jr = (N_PLANES * np.arange(p_per)[None, :] + np.arange(N_PLANES)[:, None]).astype(np.float64)
        b = 2.0 * np.pi * ((jr[:, :, None] * cols[None, None, :]) % n) / n
        ca, sa = jnp.asarray(np.cos(a), F32)[None, :, None, :], jnp.asarray(np.sin(a), F32)[None, :, None, :]
        cb, sb = jnp.asarray(np.cos(b), F32)[:, None, :, :], jnp.asarray(np.sin(b), F32)[:, None, :, :]
        cos = (ca * cb - sa * sb).reshape(n, n).astype(BF16)
        msin = (-(sa * cb + ca * sb)).reshape(n, n).astype(BF16)
        return cos, msin

    return build(cols_nat), build(cols_plane)


def _fold_kernel(c_ref, s_ref, w_ref, o_ref):
    w = w_ref[...]
    o_ref[:, :GROUP_DIM] = jnp.dot(c_ref[...], w, preferred_element_type=F32,
                                   precision=lax.Precision.HIGHEST).astype(BF16)
    o_ref[:, GROUP_DIM:] = jnp.dot(s_ref[...], w, preferred_element_type=F32,
                                   precision=lax.Precision.HIGHEST).astype(BF16)


def _fold_channel_dft(w_fourier):
    n = GROUP_DIM
    idx = (np.arange(n)[:, None] * np.arange(n)[None, :]) % n
    ang = 2.0 * np.pi * idx / n
    cos_c = jnp.asarray(np.cos(ang), F32)
    sin_c = jnp.asarray(np.sin(ang), F32)
    return pl.pallas_call(
        _fold_kernel,
        grid=(N_GROUPS,),
        in_specs=[
            pl.BlockSpec((n, n), lambda g: (0, 0)),
            pl.BlockSpec((n, n), lambda g: (0, 0)),
            pl.BlockSpec((None, n, n), lambda g: (g, 0, 0)),
        ],
        out_specs=pl.BlockSpec((None, n, 2 * n), lambda g: (g, 0, 0)),
        out_shape=jax.ShapeDtypeStruct((N_GROUPS, n, 2 * n), BF16),
        compiler_params=_params(("parallel",), 16),
        name="fold_channel_dft",
    )(cos_c, sin_c, w_fourier)


def _seq_dft_kernel(c_ref, ms_ref, u_ref, w_ref, *refs, norm, both):
    n_out = 2 if both else 1
    out_refs, (va_ref, vb_ref) = refs[:n_out], refs[n_out:]
    i = pl.program_id(2)

    @pl.when(i == 0)
    def _():
        for blk in range(SEG // PLANE_ROWS):
            rows = slice(blk * PLANE_ROWS, (blk + 1) * PLANE_ROWS)
            for g in range(N_GROUPS):
                cols = slice(g * GROUP_DIM, (g + 1) * GROUP_DIM)
                v = jnp.dot(u_ref[blk, :, cols], w_ref[g], preferred_element_type=F32)
                va_ref[rows, cols] = v[:, :GROUP_DIM].astype(BF16)
                vb_ref[rows, cols] = v[:, GROUP_DIM:].astype(BF16)

    c = c_ref[...]
    ms = ms_ref[...]
    va = va_ref[...]
    vb = vb_ref[...]
    y1 = jnp.dot(c, va, preferred_element_type=F32) + jnp.dot(ms, vb, preferred_element_type=F32)
    if both:
        y2 = jnp.dot(c, vb, preferred_element_type=F32) - jnp.dot(ms, va, preferred_element_type=F32)
        out_refs[0][...] = y1
        out_refs[1][...] = y2
    else:
        out_refs[0][...] = (y1 * norm).astype(BF16)


def _seq_dft(z4, wcs, tables, B, nseg, norm):
    fw = FOURIER_WIDTH
    tm = 512
    both = nseg > 1
    u_col = IN_COLS // fw - 1
    if both:
        assert nseg == N_PLANES
        n_cls = N_PLANES
        u_spec = pl.BlockSpec((None, nseg, None, PLANE_ROWS, fw), lambda b, s, i: (b, 0, s, 0, u_col))
    else:
        n_cls = 1
        u_spec = pl.BlockSpec((None, None, N_PLANES, PLANE_ROWS, fw), lambda b, s, i: (b, 0, 0, 0, u_col))
    m_spec = pl.BlockSpec((tm, SEG), lambda b, s, i: (i, 0))
    o_spec = pl.BlockSpec((None, None, tm, fw), lambda b, s, i: (b, s, i, 0))
    odt = F32 if both else BF16
    outs = pl.pallas_call(
        functools.partial(_seq_dft_kernel, norm=norm, both=both),
        grid=(B, n_cls, SEG // tm),
        in_specs=[m_spec, m_spec, u_spec,
                  pl.BlockSpec((N_GROUPS, GROUP_DIM, 2 * GROUP_DIM), lambda b, s, i: (0, 0, 0))],
        out_specs=[o_spec] * (2 if both else 1),
        out_shape=[jax.ShapeDtypeStruct((B, n_cls, SEG, fw), odt)] * (2 if both else 1),
        scratch_shapes=[pltpu.VMEM((SEG, fw), BF16), pltpu.VMEM((SEG, fw), BF16)],
        compiler_params=_params(("parallel", "parallel", "arbitrary"), 48),
        name="four_seq_r4" if both else "four_seq_r1",
    )(tables[0], tables[1], z4, wcs)
    return outs if both else outs[0]


def _twiddle_kernel(y1_ref, y2_ref, tw_ref, f_ref, *, radix, norm):
    tw = tw_ref[...]
    acc = None
    for s in range(radix):
        term = tw[:, s:s + 1] * y1_ref[s] - tw[:, radix + s:radix + s + 1] * y2_ref[s]
        acc = term if acc is None else acc + term
    f_ref[...] = (acc * norm).astype(BF16)


def _twiddle_combine(y1, y2, norm):
    B, radix, _, fw = y1.shape
    nseg = radix
    S = nseg * SEG
    tm = 256
    g, r, m = np.meshgrid(np.arange(nseg), np.arange(N_PLANES), np.arange(PLANE_ROWS), indexing="ij")
    j = (SEG * g + N_PLANES * m + r).astype(np.float64)[..., None]
    ang = 2.0 * np.pi * ((j * np.arange(radix)) % S) / S
    tw = np.zeros((nseg, N_PLANES, PLANE_ROWS, LANES), np.float32)
    tw[..., :radix] = np.cos(ang)
    tw[..., radix:2 * radix] = np.sin(ang)
    nb = PLANE_ROWS // tm
    y_spec = pl.BlockSpec((None, radix, tm, fw), lambda b, r, i, g: (b, 0, r * nb + i, 0))
    return pl.pallas_call(
        functools.partial(_twiddle_kernel, radix=radix, norm=norm),
        grid=(B, N_PLANES, nb, nseg),
        in_specs=[y_spec, y_spec, pl.BlockSpec((None, None, tm, LANES), lambda b, r, i, g: (g, r, i, 0))],
        out_specs=pl.BlockSpec((None, None, None, tm, fw), lambda b, r, i, g: (b, g, r, i, 0)),
        out_shape=jax.ShapeDtypeStruct((B, nseg, N_PLANES, PLANE_ROWS, fw), BF16),
        compiler_params=_params(("parallel", "parallel", "parallel", "arbitrary"), 48),
        name="four_twiddle",
    )(y1, y2, jnp.asarray(tw))


def _fourier(z4s, wcs, tables_nat, tables_plane, B, nseg):
    G = B * nseg
    fw = FOURIER_WIDTH
    norm = 1.0 / math.sqrt(nseg * SEG * GROUP_DIM)
    if nseg == 1:
        f = _seq_dft(z4s, wcs, tables_plane, B, nseg, norm)
    else:
        y1, y2 = _seq_dft(z4s, wcs, tables_nat, B, nseg, norm)
        f = _twiddle_combine(y1, y2, norm)
    return f.reshape(G, N_PLANES, PLANE_ROWS, fw)


def _out_proj_kernel(o1_ref, o4_ref, o16_ref, l1_ref, l4_ref, l16_ref, f_ref, x_ref, mod_ref,
                     ga_ref, gf_ref, gpm_ref, gpl_ref, e2_ref, w_ref, x1_ref, h2_ref,
                     xslab, wcat, wbc, mixed, ybuf, *, tm):
    aw = ATTN_WIDTH
    pr = tm // N_PLANES
    for s in range(D_SLABS):
        xslab[s] = x_ref[:, _lane_slab(s)]

    for r in range(N_PLANES):
        rows = slice(r * pr, (r + 1) * pr)
        l1, l4, l16 = l1_ref[r], l4_ref[r], l16_ref[r]
        top = jnp.maximum(jnp.maximum(l1, l4), l16)
        e1, e4, e16 = jnp.exp(l1 - top), jnp.exp(l4 - top), jnp.exp(l16 - top)
        inv = 1.0 / (e1 + e4 + e16)
        for b, e in enumerate((e1, e4, e16)):
            w = e * inv
            hi = w.astype(BF16)
            wcat[b, rows, 0:LANES] = hi
            wcat[b, rows, LANES:] = (w - hi.astype(F32)).astype(BF16)
    for b in range(3):
        wbc[b] = jnp.dot(wcat[b], e2_ref[...], preferred_element_type=F32)

    for r in range(N_PLANES):
        rows = slice(r * pr, (r + 1) * pr)
        o16 = jnp.concatenate([o16_ref[r, h] for h in range(N_HEADS)], axis=1)
        a = (wbc[0, rows, :] * o1_ref[r].astype(F32) + wbc[1, rows, :] * o4_ref[r].astype(F32)
             + wbc[2, rows, :] * o16)
        mixed[rows, :aw] = _rms_rows(a, ga_ref[...]).astype(BF16)
        mixed[rows, aw:] = _rms_rows(f_ref[r].astype(F32), gf_ref[...]).astype(BF16)

    ybuf[...] = jnp.dot(mixed[...], w_ref[...], preferred_element_type=F32)

    gate1 = mod_ref[2:3, :]
    shift2 = mod_ref[3:4, :]
    scale2 = 1.0 + mod_ref[4:5, :]
    for r in range(N_PLANES):
        rows = slice(r * pr, (r + 1) * pr)
        xr = jnp.concatenate([xslab[s, pl.ds(r, pr, stride=N_PLANES), :] for s in range(D_SLABS)], axis=1)
        x1 = xr + gate1 * _rms_rows(ybuf[rows, :], gpm_ref[...])
        x1_ref[r] = x1
        h2_ref[r] = (_rms_rows(x1, gpl_ref[...]) * scale2 + shift2).astype(BF16)


def _lane_broadcast_matrix():
    e = np.zeros((2 * LANES, ATTN_WIDTH), np.float32)
    for h in range(N_HEADS):
        e[h, h * HEAD_DIM:(h + 1) * HEAD_DIM] = 1.0
        e[LANES + h, h * HEAD_DIM:(h + 1) * HEAD_DIM] = 1.0
    return jnp.asarray(e, BF16)


def _out_proj(o1, o4, o16, l1, l4, l16, f, xg, mod, nseg, g_attn_out, g_fourier_out, g_post_mix, g_pre_mlp, w_out_bf16):
    G = xg.shape[0]
    tm = 256
    pr = tm // N_PLANES
    aw, fw = ATTN_WIDTH, FOURIER_WIDTH
    plane = lambda g, i: (g, 0, i, 0)
    const2 = lambda g, i: (0, 0)
    return pl.pallas_call(
        functools.partial(_out_proj_kernel, tm=tm),
        grid=(G, SEG // tm),
        in_specs=[
            pl.BlockSpec((None, N_PLANES, pr, aw), plane),
            pl.BlockSpec((None, N_PLANES, pr, aw), plane),
            pl.BlockSpec((None, N_PLANES, N_HEADS, pr, LANES), lambda g, i: (g, 0, 0, i, 0)),
            pl.BlockSpec((None, N_PLANES, pr, LANES), plane),
            pl.BlockSpec((None, N_PLANES, pr, LANES), plane),
            pl.BlockSpec((None, N_PLANES, pr, LANES), plane),
            pl.BlockSpec((None, N_PLANES, pr, fw), plane),
            pl.BlockSpec((None, tm, D_MODEL), lambda g, i: (g, i, 0)),
            pl.BlockSpec((None, N_MOD, D_MODEL), lambda g, i: (g // nseg, 0, 0)),
            pl.BlockSpec((1, aw), const2),
            pl.BlockSpec((1, fw), const2),
            pl.BlockSpec((1, D_MODEL), const2),
            pl.BlockSpec((1, D_MODEL), const2),
            pl.BlockSpec((2 * LANES, aw), const2),
            pl.BlockSpec((aw + fw, D_MODEL), const2),
        ],
        out_specs=[pl.BlockSpec((None, N_PLANES, pr, D_MODEL), plane)] * 2,
        out_shape=[jax.ShapeDtypeStruct((G, N_PLANES, PLANE_ROWS, D_MODEL), F32),
                   jax.ShapeDtypeStruct((G, N_PLANES, PLANE_ROWS, D_MODEL), BF16)],
        scratch_shapes=[
            pltpu.VMEM((D_SLABS, tm, LANES), F32),
            pltpu.VMEM((3, tm, 2 * LANES), BF16),
            pltpu.VMEM((3, tm, aw), F32),
            pltpu.VMEM((tm, aw + fw), BF16),
            pltpu.VMEM((tm, D_MODEL), F32),
        ],
        compiler_params=_params(("parallel", "parallel"), 56),
        name="out_proj",
    )(o1, o4, o16, l1, l4, l16, f, xg, mod, g_attn_out.reshape(1, aw), g_fourier_out.reshape(1, fw),
      g_post_mix.reshape(1, D_MODEL), g_pre_mlp.reshape(1, D_MODEL), _lane_broadcast_matrix(), w_out_bf16)


def _mlp_kernel(h_ref, x1_ref, mod_ref, g_ref, w1_ref, w2_ref, out_ref, acc_ref, oslab, *, tm, n_j):
    j = pl.program_id(2)
    pr = tm // N_PLANES

    @pl.when(j == 0)
    def _():
        acc_ref[...] = jnp.zeros_like(acc_ref)

    h = h_ref[...].reshape(tm, D_MODEL)
    hid = jnp.dot(h, w1_ref[...], preferred_element_type=F32)
    hid = jnp.square(jnp.maximum(hid, 0.0)).astype(BF16)
    acc_ref[...] += jnp.dot(hid, w2_ref[...], preferred_element_type=F32)

    @pl.when(j == n_j - 1)
    def _():
        gate2 = mod_ref[5:6, :]
        rc = 64
        for r in range(N_PLANES):
            def body(c, carry):
                m0 = pl.multiple_of(c * rc, rc)
                y = acc_ref[pl.ds(r * pr + m0, rc), :]
                v = x1_ref[r, pl.ds(m0, rc), :] + gate2 * _rms_rows(y, g_ref[...])
                for s in range(D_SLABS):
                    oslab[s, pl.ds(r + N_PLANES * m0, rc, stride=N_PLANES), :] = v[:, _lane_slab(s)]
                return carry

            lax.fori_loop(0, pr // rc, body, 0)
        for s in range(D_SLABS):
            out_ref[:, _lane_slab(s)] = oslab[s]


def _mlp(h2, x1, mod, nseg, g_post_mlp, w1_blocks, w2_bf16):
    G = x1.shape[0]
    tm = 512
    n_j, _, tf = w1_blocks.shape
    pr = tm // N_PLANES
    plane = lambda g, i, j: (g, 0, i, 0)
    return pl.pallas_call(
        functools.partial(_mlp_kernel, tm=tm, n_j=n_j),
        grid=(G, SEG // tm, n_j),
        in_specs=[
            pl.BlockSpec((None, N_PLANES, pr, D_MODEL), plane),
            pl.BlockSpec((None, N_PLANES, pr, D_MODEL), plane),
            pl.BlockSpec((None, N_MOD, D_MODEL), lambda g, i, j: (g // nseg, 0, 0)),
            pl.BlockSpec((1, D_MODEL), lambda g, i, j: (0, 0)),
            pl.BlockSpec((None, D_MODEL, tf), lambda g, i, j: (j, 0, 0)),
            pl.BlockSpec((tf, D_MODEL), lambda g, i, j: (j, 0)),
        ],
        out_specs=pl.BlockSpec((None, tm, D_MODEL), lambda g, i, j: (g, i, 0)),
        out_shape=jax.ShapeDtypeStruct((G, SEG, D_MODEL), F32),
        scratch_shapes=[pltpu.VMEM((tm, D_MODEL), F32), pltpu.VMEM((D_SLABS, tm, LANES), F32)],
        compiler_params=_params(("parallel", "parallel", "arbitrary"), 56),
        name="mlp",
    )(h2, x1, mod, g_post_mlp.reshape(1, D_MODEL), w1_blocks, w2_bf16)


def _column_blocks(w, tn):
    k, n = w.shape
    return w.astype(BF16).reshape(k, n // tn, tn).swapaxes(0, 1)


def _layer(x, mod, p):
    B, S, _ = x.shape
    assert S % SEG == 0
    nseg = S // SEG
    G = B * nseg
    xg = x.reshape(G, SEG, D_MODEL)
    z4, z16 = _in_proj(xg, mod, nseg, p["g_pre_mix"], p["w_in"])
    z4s = z4.reshape(B, nseg, N_PLANES, PLANE_ROWS, IN_COLS)
    z16s = z16.reshape(B, nseg, 4, N_PLANES, CLASS_ROWS, 3 * ATTN_WIDTH)
    o1, l1 = _attn_d1(z4s, B, nseg)
    o4, l4 = _attn_d4(z4s, B, nseg)
    o16, l16 = _attn_d16(z16s, B, nseg)
    f = _fourier(z4s, p["wcs"], p["tables_nat"], p["tables_plane"], B, nseg)
    pshape = (G, N_PLANES, PLANE_ROWS)
    x1, h2 = _out_proj(
        o1.reshape(pshape + (ATTN_WIDTH,)), o4.reshape(pshape + (ATTN_WIDTH,)),
        o16.reshape((G, N_PLANES, N_HEADS, PLANE_ROWS, LANES)),
        l1.reshape(pshape + (LANES,)), l4.reshape(pshape + (LANES,)), l16.reshape(pshape + (LANES,)),
        f, xg, mod, nseg, p["g_attn_out"], p["g_fourier_out"], p["g_post_mix"], p["g_pre_mlp"], p["w_out"])
    out = _mlp(h2, x1, mod, nseg, p["g_post_mlp"], p["w_mlp_in"], p["w_mlp_out"])
    return out.reshape(B, S, D_MODEL)


def kernel(x_prompt, x_sample, c_prompt, c_sample, w_ada, b_ada, g_pre_mix, w_in, g_attn_out, w_fourier,
           g_fourier_out, w_out, g_post_mix, g_pre_mlp, w_mlp_in, w_mlp_out, g_post_mlp):
    depth = w_ada.shape[0]
    nb_p, nb_s = c_prompt.shape[0], c_sample.shape[0]
    pad = (-(nb_p + nb_s)) % 8
    c_all = jnp.concatenate([c_prompt, c_sample, jnp.zeros((pad, D_MODEL), F32)], axis=0)
    tables_nat, tables_plane = _dft_tables()
    xp, xs = x_prompt, x_sample
    for l in range(depth):
        mod = _modulation(c_all, w_ada[l], b_ada[l])
        mod_p = mod[:nb_p].reshape(nb_p, N_MOD, D_MODEL)
        mod_s = mod[nb_p:nb_p + nb_s].reshape(nb_s, N_MOD, D_MODEL)
        p = dict(
            g_pre_mix=g_pre_mix[l], w_in=_column_blocks(w_in[l], 1024), g_attn_out=g_attn_out[l],
            g_fourier_out=g_fourier_out[l], w_out=w_out[l].astype(BF16), g_post_mix=g_post_mix[l],
            g_pre_mlp=g_pre_mlp[l], w_mlp_in=_column_blocks(w_mlp_in[l], 1024), w_mlp_out=w_mlp_out[l].astype(BF16),
            g_post_mlp=g_post_mlp[l], wcs=_fold_channel_dft(w_fourier[l]),
            tables_nat=tables_nat, tables_plane=tables_plane,
        )
        xp = _layer(xp, mod_p, p)
        xs = _layer(xs, mod_s, p)
    return (xp, xs)
```

```python
import functools
import math

import numpy as np
import jax
import jax.numpy as jnp
from jax import lax
from jax.experimental import pallas as pl
from jax.experimental.pallas import tpu as pltpu

D_MODEL = 2048
ATTN_WIDTH = 1024
FOURIER_WIDTH = 1024
HEAD_DIM = 128
N_HEADS = 8
N_GROUPS = 4
GROUP_DIM = 256
D_FF = 8192
IN_COLS = 3 * ATTN_WIDTH + FOURIER_WIDTH
DILATED_BRANCHES = ((128, 1), (512, 4), (2048, 16))
HALF = 64
N_MOD = 6
RMS_EPS = 1e-6
NEG_INF = -1e30

SEG = 2048
N_PLANES = 4
PLANE_ROWS = SEG // N_PLANES
N_CLASSES = 16
CLASS_ROWS = SEG // N_CLASSES
LANES = 128
D_SLABS = D_MODEL // LANES
Q_SUB = 128
K_SUB = 256
K_PAD = K_SUB - (Q_SUB + 2 * HALF)

F32 = jnp.float32
BF16 = jnp.bfloat16
MIB = 1024 * 1024

assert DILATED_BRANCHES == ((128, 1), (512, 4), (2048, 16))
assert all(w // (2 * d) == HALF for w, d in DILATED_BRANCHES)


def _params(semantics, vmem_mib):
    return pltpu.CompilerParams(dimension_semantics=semantics, vmem_limit_bytes=vmem_mib * MIB)


def _rms_rows(v, gain):
    ms = jnp.mean(v * v, axis=-1, keepdims=True)
    return v * lax.rsqrt(ms + RMS_EPS) * gain


def _lane_slab(s):
    return slice(s * LANES, (s + 1) * LANES)


def _mod_kernel(c_ref, w_ref, b_ref, o_ref):
    c = c_ref[...]
    a = (c * jax.nn.sigmoid(c)).astype(BF16)
    o_ref[...] = jnp.dot(a, w_ref[...].astype(BF16), preferred_element_type=F32) + b_ref[...]


def _modulation(c_all, w_ada, b_ada):
    rows = c_all.shape[0]
    n = w_ada.shape[1]
    tn = 1024
    return pl.pallas_call(
        _mod_kernel,
        grid=(n // tn,),
        in_specs=[
            pl.BlockSpec((rows, D_MODEL), lambda j: (0, 0)),
            pl.BlockSpec((D_MODEL, tn), lambda j: (0, j)),
            pl.BlockSpec((1, tn), lambda j: (0, j)),
        ],
        out_specs=pl.BlockSpec((rows, tn), lambda j: (0, j)),
        out_shape=jax.ShapeDtypeStruct((rows, n), F32),
        compiler_params=_params(("parallel",), 40),
        name="mod",
    )(c_all, w_ada, b_ada.reshape(1, n))


def _in_proj_kernel(x_ref, mod_ref, g_ref, w_ref, z4_ref, z16_ref, h_ref, hslab, aslab, *, tm, tn, nb):
    j = pl.program_id(2)
    pr = tm // N_PLANES
    pb = nb // N_PLANES

    @pl.when(j == 0)
    def _():
        shift = mod_ref[0:1, :]
        scale1 = 1.0 + mod_ref[1:2, :]
        gain = g_ref[...]

        def block(c, carry):
            n0 = pl.multiple_of(c * nb, nb)
            rq = 16
            for q in range(nb // rq):
                h = _rms_rows(x_ref[pl.ds(n0 + q * rq, rq), :], gain) * scale1 + shift
                for s in range(D_SLABS):
                    hslab[s, q * rq:(q + 1) * rq, :] = h[:, _lane_slab(s)]
            m0 = pl.multiple_of(c * pb, pb)
            for r in range(N_PLANES):
                for s in range(D_SLABS):
                    h_ref[pl.ds(r * pr + m0, pb), _lane_slab(s)] = (
                        hslab[s, pl.ds(r, pb, stride=N_PLANES), :].astype(BF16))
            return carry

        lax.fori_loop(0, tm // nb, block, 0)

    acc = jnp.dot(h_ref[...], w_ref[...], preferred_element_type=F32)
    acc = acc * jnp.where(j == 1, HEAD_DIM ** -0.5, 1.0).astype(F32)
    for r in range(N_PLANES):
        z4_ref[r] = acc[r * pr:(r + 1) * pr].astype(BF16)
    for s in range(tn // LANES):
        aslab[s] = acc[:, _lane_slab(s)]
    cr = pr // 4
    for r in range(N_PLANES):
        for a in range(4):
            for s in range(tn // LANES):
                z16_ref[r + 4 * a, :, _lane_slab(s)] = (
                    aslab[s, pl.ds(r * pr + a, cr, stride=4), :].astype(BF16))


def _in_proj(xg, mod, nseg, g_pre_mix, w_in_bf16):
    G = xg.shape[0]
    tm, tn = 1024, 1024
    n_j = IN_COLS // tn
    return pl.pallas_call(
        functools.partial(_in_proj_kernel, tm=tm, tn=tn, nb=256),
        grid=(G, SEG // tm, n_j),
        in_specs=[
            pl.BlockSpec((None, tm, D_MODEL), lambda g, i, j: (g, i, 0)),
            pl.BlockSpec((None, N_MOD, D_MODEL), lambda g, i, j: (g // nseg, 0, 0)),
            pl.BlockSpec((1, D_MODEL), lambda g, i, j: (0, 0)),
            pl.BlockSpec((D_MODEL, tn), lambda g, i, j: (0, (j + n_j - 1) % n_j)),
        ],
        out_specs=[
            pl.BlockSpec((None, N_PLANES, tm // N_PLANES, tn), lambda g, i, j: (g, 0, i, (j + n_j - 1) % n_j)),
            pl.BlockSpec((None, N_CLASSES, tm // N_CLASSES, tn), lambda g, i, j: (g, 0, i, jnp.maximum(j, 1) - 1)),
        ],
        out_shape=[
            jax.ShapeDtypeStruct((G, N_PLANES, PLANE_ROWS, IN_COLS), BF16),
            jax.ShapeDtypeStruct((G, N_CLASSES, CLASS_ROWS, 3 * ATTN_WIDTH), BF16),
        ],
        scratch_shapes=[
            pltpu.VMEM((tm, D_MODEL), BF16),
            pltpu.VMEM((D_SLABS, 256, LANES), F32),
            pltpu.VMEM((tn // LANES, tm, LANES), F32),
        ],
        compiler_params=_params(("parallel", "parallel", "arbitrary"), 56),
        name="in_proj",
    )(xg, mod, g_pre_mix.reshape(1, D_MODEL), w_in_bf16)


def _softmax_pv(s, vh):
    m = jnp.max(s, axis=-1, keepdims=True)
    p = jnp.exp(s - m)
    den = jnp.sum(p, axis=-1, keepdims=True)
    o = jnp.dot(p.astype(BF16), vh, preferred_element_type=F32) / den
    return o, m + jnp.log(den)


def _scores(qh, kh, bias):
    return lax.dot_general(qh, kh, (((1,), (1,)), ((), ())), preferred_element_type=F32) + bias


def _alibi_tile(rel, dil):
    slopes = 2.0 ** (-8.0 * (np.arange(N_HEADS, dtype=np.float64) + 1.0) / N_HEADS)
    bias = -(slopes * dil)[:, None, None] * rel[None]
    return jnp.asarray(np.where(rel[None] <= HALF, bias, NEG_INF), dtype=F32)


def _band_bias(dil):
    rel = np.abs((np.arange(K_SUB)[None, :] - HALF) - np.arange(Q_SUB)[:, None]).astype(np.float64)
    return _alibi_tile(rel, dil)


def _plane_bias():
    qr, qm = np.divmod(np.arange(Q_SUB), Q_SUB // N_PLANES)
    kr, km = np.divmod(np.arange(K_SUB), K_SUB // N_PLANES)
    rel = np.abs(N_PLANES * (km[None, :] - HALF // N_PLANES - qm[:, None]) + (kr[None, :] - qr[:, None])).astype(np.float64)
    return _alibi_tile(rel, 1)


def _attn_class_kernel(q_ref, kp_ref, kc_ref, kn_ref, vp_ref, vc_ref, vn_ref, bias_ref,
                       o_ref, lse_ref, kext, vext, *, lq, n_i):
    i = pl.program_id(1)
    n_sub = lq // Q_SUB
    col = lax.broadcasted_iota(jnp.int32, (Q_SUB, K_SUB), 1)
    lane = lax.broadcasted_iota(jnp.int32, (Q_SUB, LANES), 1)

    if K_PAD:
        kext[2 * HALF + lq:, :] = jnp.zeros((K_PAD, ATTN_WIDTH), BF16)
        vext[2 * HALF + lq:, :] = jnp.zeros((K_PAD, ATTN_WIDTH), BF16)

    def plane(r, carry):
        kext[0:HALF, :] = kp_ref[r]
        kext[HALF:HALF + lq, :] = kc_ref[r]
        kext[HALF + lq:2 * HALF + lq, :] = kn_ref[r]
        vext[0:HALF, :] = vp_ref[r]
        vext[HALF:HALF + lq, :] = vc_ref[r]
        vext[HALF + lq:2 * HALF + lq, :] = vn_ref[r]
        for sb in range(n_sub):
            r0 = sb * Q_SUB
            lse_tile = jnp.zeros((Q_SUB, LANES), F32)
            for h in range(N_HEADS):
                hc = _lane_slab(h)
                s = _scores(q_ref[r, r0:r0 + Q_SUB, hc], kext[r0:r0 + K_SUB, hc], bias_ref[h])
                if sb == 0:
                    s = jnp.where((col >= HALF) | (i > 0), s, NEG_INF)
                if sb == n_sub - 1:
                    s = jnp.where((col < Q_SUB + HALF) | (i < n_i - 1), s, NEG_INF)
                o, lse = _softmax_pv(s, vext[r0:r0 + K_SUB, hc])
                o_ref[r, r0:r0 + Q_SUB, hc] = o.astype(BF16)
                lse_tile = jnp.where(lane == h, lse, lse_tile)
            lse_ref[r, r0:r0 + Q_SUB, :] = lse_tile
        return carry

    lax.fori_loop(0, N_PLANES, plane, 0)


def _attn_d4(z4, B, nseg):
    aw = ATTN_WIDTH
    lq = PLANE_ROWS
    hb = lq // HALF

    def cur(part):
        return pl.BlockSpec((None, None, N_PLANES, lq, aw), lambda b, i: (b, i, 0, 0, part))

    def prev(part):
        return pl.BlockSpec((None, None, N_PLANES, HALF, aw), lambda b, i: (b, jnp.maximum(i - 1, 0), 0, hb - 1, part))

    def nxt(part):
        return pl.BlockSpec((None, None, N_PLANES, HALF, aw), lambda b, i: (b, jnp.minimum(i + 1, nseg - 1), 0, 0, part))

    return pl.pallas_call(
        functools.partial(_attn_class_kernel, lq=lq, n_i=nseg),
        grid=(B, nseg),
        in_specs=[cur(0), prev(1), cur(1), nxt(1), prev(2), cur(2), nxt(2),
                  pl.BlockSpec((N_HEADS, Q_SUB, K_SUB), lambda b, i: (0, 0, 0))],
        out_specs=[
            pl.BlockSpec((None, None, N_PLANES, lq, aw), lambda b, i: (b, i, 0, 0, 0)),
            pl.BlockSpec((None, None, N_PLANES, lq, LANES), lambda b, i: (b, i, 0, 0, 0)),
        ],
        out_shape=[
            jax.ShapeDtypeStruct((B, nseg, N_PLANES, PLANE_ROWS, aw), BF16),
            jax.ShapeDtypeStruct((B, nseg, N_PLANES, PLANE_ROWS, LANES), F32),
        ],
        scratch_shapes=[pltpu.VMEM((lq + 2 * HALF + K_PAD, aw), BF16)] * 2,
        compiler_params=_params(("parallel", "arbitrary"), 52),
        name="attn_d4",
    )(z4, z4, z4, z4, z4, z4, z4, _band_bias(4))


def _attn_d16_kernel(q_ref, kp_ref, kc_ref, kn_ref, vp_ref, vc_ref, vn_ref, bias_ref, o_ref, lse_ref, *, n_i, n_pl):
    i = pl.program_id(2)
    col = lax.broadcasted_iota(jnp.int32, (Q_SUB, K_SUB), 1)
    lane = lax.broadcasted_iota(jnp.int32, (Q_SUB, LANES), 1)

    n_sub = CLASS_ROWS // Q_SUB
    pad = [jnp.zeros((K_PAD, HEAD_DIM), BF16)] if K_PAD else []

    def plane(r, carry):
        for a in range(4):
            for sb in range(n_sub):
                r0 = sb * Q_SUB
                lse_tile = jnp.zeros((Q_SUB, LANES), F32)
                for h in range(N_HEADS):
                    hc = _lane_slab(h)
                    kh = jnp.concatenate([kp_ref[a, r, :, hc], kc_ref[a, r, :, hc], kn_ref[a, r, :, hc]] + pad,
                                         axis=0)[r0:r0 + K_SUB]
                    vh = jnp.concatenate([vp_ref[a, r, :, hc], vc_ref[a, r, :, hc], vn_ref[a, r, :, hc]] + pad,
                                         axis=0)[r0:r0 + K_SUB]
                    s = _scores(q_ref[a, r, r0:r0 + Q_SUB, hc], kh, bias_ref[h])
                    if sb == 0:
                        s = jnp.where((col >= HALF) | (i > 0), s, NEG_INF)
                    if sb == n_sub - 1:
                        s = jnp.where((col < Q_SUB + HALF) | (i < n_i - 1), s, NEG_INF)
                    o, lse = _softmax_pv(s, vh)
                    o_ref[r, h, pl.ds(a + 4 * r0, Q_SUB, stride=4), :] = o
                    lse_tile = jnp.where(lane == h, lse, lse_tile)
                lse_ref[r, pl.ds(a + 4 * r0, Q_SUB, stride=4), :] = lse_tile
        return carry

    lax.fori_loop(0, n_pl, plane, 0)


def _attn_d16(z16, B, nseg):
    aw = ATTN_WIDTH
    lq = CLASS_ROWS
    hb = lq // HALF
    n_pl = 2

    def cur(part):
        return pl.BlockSpec((None, None, 4, n_pl, lq, aw), lambda b, r, i: (b, i, 0, r, 0, part))

    def prev(part):
        return pl.BlockSpec((None, None, 4, n_pl, HALF, aw),
                            lambda b, r, i: (b, jnp.maximum(i - 1, 0), 0, r, hb - 1, part))

    def nxt(part):
        return pl.BlockSpec((None, None, 4, n_pl, HALF, aw),
                            lambda b, r, i: (b, jnp.minimum(i + 1, nseg - 1), 0, r, 0, part))

    return pl.pallas_call(
        functools.partial(_attn_d16_kernel, n_i=nseg, n_pl=n_pl),
        grid=(B, N_PLANES // n_pl, nseg),
        in_specs=[cur(0), prev(1), cur(1), nxt(1), prev(2), cur(2), nxt(2),
                  pl.BlockSpec((N_HEADS, Q_SUB, K_SUB), lambda b, r, i: (0, 0, 0))],
        out_specs=[
            pl.BlockSpec((None, None, n_pl, N_HEADS, PLANE_ROWS, LANES), lambda b, r, i: (b, i, r, 0, 0, 0)),
            pl.BlockSpec((None, None, n_pl, PLANE_ROWS, LANES), lambda b, r, i: (b, i, r, 0, 0)),
        ],
        out_shape=[
            jax.ShapeDtypeStruct((B, nseg, N_PLANES, N_HEADS, PLANE_ROWS, LANES), F32),
            jax.ShapeDtypeStruct((B, nseg, N_PLANES, PLANE_ROWS, LANES), F32),
        ],
        compiler_params=_params(("parallel", "parallel", "arbitrary"), 40),
        name="attn_d16",
    )(z16, z16, z16, z16, z16, z16, z16, _band_bias(16))


def _attn_d1_kernel(q_ref, kp_ref, kc_ref, kn_ref, vp_ref, vc_ref, vn_ref, bias_ref,
                    o_ref, lse_ref, kext, vext, *, mq, n_t):
    t = pl.program_id(1)
    qm = Q_SUB // N_PLANES
    km = K_SUB // N_PLANES
    hm = HALF // N_PLANES
    pm = K_PAD // N_PLANES
    for r in range(N_PLANES):
        kext[r, 0:hm, :] = kp_ref[r]
        kext[r, hm:hm + mq, :] = kc_ref[r]
        kext[r, hm + mq:2 * hm + mq, :] = kn_ref[r]
        vext[r, 0:hm, :] = vp_ref[r]
        vext[r, hm:hm + mq, :] = vc_ref[r]
        vext[r, hm + mq:2 * hm + mq, :] = vn_ref[r]
        if pm:
            kext[r, 2 * hm + mq:, :] = jnp.zeros((pm, ATTN_WIDTH), BF16)
            vext[r, 2 * hm + mq:, :] = jnp.zeros((pm, ATTN_WIDTH), BF16)

    n_sub = mq // qm
    colm = lax.broadcasted_iota(jnp.int32, (Q_SUB, K_SUB), 1) & (km - 1)
    lane = lax.broadcasted_iota(jnp.int32, (Q_SUB, LANES), 1)
    for sb in range(n_sub):
        m1 = sb * qm
        lse_tile = jnp.zeros((Q_SUB, LANES), F32)
        for h in range(N_HEADS):
            hc = _lane_slab(h)
            qh = jnp.concatenate([q_ref[r, m1:m1 + qm, hc] for r in range(N_PLANES)], axis=0)
            kh = jnp.concatenate([kext[r, m1:m1 + km, hc] for r in range(N_PLANES)], axis=0)
            vh = jnp.concatenate([vext[r, m1:m1 + km, hc] for r in range(N_PLANES)], axis=0)
            s = _scores(qh, kh, bias_ref[h])
            if sb == 0:
                s = jnp.where((colm >= hm) | (t > 0), s, NEG_INF)
            if sb == n_sub - 1:
                s = jnp.where((colm < qm + hm) | (t < n_t - 1), s, NEG_INF)
            o, lse = _softmax_pv(s, vh)
            for r in range(N_PLANES):
                o_ref[r, m1:m1 + qm, hc] = o[r * qm:(r + 1) * qm].astype(BF16)
            lse_tile = jnp.where(lane == h, lse, lse_tile)
        for r in range(N_PLANES):
            lse_ref[r, m1:m1 + qm, :] = lse_tile[r * qm:(r + 1) * qm]


def _attn_d1(z4, B, nseg):
    aw = ATTN_WIDTH
    mq = 256
    hm = HALF // N_PLANES
    per_seg = PLANE_ROWS // mq
    hb_step = mq // hm
    hb_seg = PLANE_ROWS // hm
    n_t = nseg * per_seg

    def cur(part):
        return pl.BlockSpec((None, None, N_PLANES, mq, aw), lambda b, t: (b, t // per_seg, 0, t % per_seg, part))

    def prev(part):
        def imap(b, t):
            q = jnp.maximum(t * hb_step - 1, 0)
            return (b, q // hb_seg, 0, q % hb_seg, part)
        return pl.BlockSpec((None, None, N_PLANES, hm, aw), imap)

    def nxt(part):
        def imap(b, t):
            q = jnp.minimum((t + 1) * hb_step, nseg * hb_seg - 1)
            return (b, q // hb_seg, 0, q % hb_seg, part)
        return pl.BlockSpec((None, None, N_PLANES, hm, aw), imap)

    return pl.pallas_call(
        functools.partial(_attn_d1_kernel, mq=mq, n_t=n_t),
        grid=(B, n_t),
        in_specs=[cur(0), prev(1), cur(1), nxt(1), prev(2), cur(2), nxt(2),
                  pl.BlockSpec((N_HEADS, Q_SUB, K_SUB), lambda b, t: (0, 0, 0))],
        out_specs=[
            pl.BlockSpec((None, None, N_PLANES, mq, aw), lambda b, t: (b, t // per_seg, 0, t % per_seg, 0)),
            pl.BlockSpec((None, None, N_PLANES, mq, LANES), lambda b, t: (b, t // per_seg, 0, t % per_seg, 0)),
        ],
        out_shape=[
            jax.ShapeDtypeStruct((B, nseg, N_PLANES, PLANE_ROWS, aw), BF16),
            jax.ShapeDtypeStruct((B, nseg, N_PLANES, PLANE_ROWS, LANES), F32),
        ],
        scratch_shapes=[pltpu.VMEM((N_PLANES, mq + 2 * hm + K_PAD // N_PLANES, aw), BF16)] * 2,
        compiler_params=_params(("parallel", "arbitrary"), 32),
        name="attn_d1",
    )(z4, z4, z4, z4, z4, z4, z4, _plane_bias())


def _dft_tables():
    n = SEG
    n1 = 32
    p_per = n1 // N_PLANES
    cols_nat = np.arange(n, dtype=np.float64)
    cols_plane = (N_PLANES * (np.arange(n) % PLANE_ROWS) + np.arange(n) // PLANE_ROWS).astype(np.float64)

    def build(cols):
        a = 2.0 * np.pi * ((np.arange(n // n1)[:, None] * n1 * cols[None, :]) % n) / n
        jr = (N_PLANES * np.arange(p_per)[None, :] + np.arange(N_PLANES)[:, None]).astype(np.float64)
        b = 2.0 * np.pi * ((jr[:, :, None] * cols[None, None, :]) % n) / n
        ca, sa = jnp.asarray(np.cos(a), F32)[None, :, None, :], jnp.asarray(np.sin(a), F32)[None, :, None, :]
        cb, sb = jnp.asarray(np.cos(b), F32)[:, None, :, :], jnp.asarray(np.sin(b), F32)[:, None, :, :]
        cos = (ca * cb - sa * sb).reshape(n, n).astype(BF16)
        msin = (-(sa * cb + ca * sb)).reshape(n, n).astype(BF16)
        return cos, msin

    return build(cols_nat), build(cols_plane)


def _fold_kernel(c_ref, s_ref, w_ref, o_ref):
    w = w_ref[...]
    o_ref[:, :GROUP_DIM] = jnp.dot(c_ref[...], w, preferred_element_type=F32,
                                   precision=lax.Precision.HIGHEST).astype(BF16)
    o_ref[:, GROUP_DIM:] = jnp.dot(s_ref[...], w, preferred_element_type=F32,
                                   precision=lax.Precision.HIGHEST).astype(BF16)


def _fold_channel_dft(w_fourier):
    n = GROUP_DIM
    idx = (np.arange(n)[:, None] * np.arange(n)[None, :]) % n
    ang = 2.0 * np.pi * idx / n
    cos_c = jnp.asarray(np.cos(ang), F32)
    sin_c = jnp.asarray(np.sin(ang), F32)
    return pl.pallas_call(
        _fold_kernel,
        grid=(N_GROUPS,),
        in_specs=[
            pl.BlockSpec((n, n), lambda g: (0, 0)),
            pl.BlockSpec((n, n), lambda g: (0, 0)),
            pl.BlockSpec((None, n, n), lambda g: (g, 0, 0)),
        ],
        out_specs=pl.BlockSpec((None, n, 2 * n), lambda g: (g, 0, 0)),
        out_shape=jax.ShapeDtypeStruct((N_GROUPS, n, 2 * n), BF16),
        compiler_params=_params(("parallel",), 16),
        name="fold_channel_dft",
    )(cos_c, sin_c, w_fourier)


def _seq_dft_kernel(c_ref, ms_ref, u_ref, w_ref, *refs, norm, both):
    n_out = 2 if both else 1
    out_refs, (va_ref, vb_ref) = refs[:n_out], refs[n_out:]
    i = pl.program_id(2)

    @pl.when(i == 0)
    def _():
        for blk in range(SEG // PLANE_ROWS):
            rows = slice(blk * PLANE_ROWS, (blk + 1) * PLANE_ROWS)
            for g in range(N_GROUPS):
                cols = slice(g * GROUP_DIM, (g + 1) * GROUP_DIM)
                v = jnp.dot(u_ref[blk, :, cols], w_ref[g], preferred_element_type=F32)
                va_ref[rows, cols] = v[:, :GROUP_DIM].astype(BF16)
                vb_ref[rows, cols] = v[:, GROUP_DIM:].astype(BF16)

    c = c_ref[...]
    ms = ms_ref[...]
    va = va_ref[...]
    vb = vb_ref[...]
    y1 = jnp.dot(c, va, preferred_element_type=F32) + jnp.dot(ms, vb, preferred_element_type=F32)
    if both:
        y2 = jnp.dot(c, vb, preferred_element_type=F32) - jnp.dot(ms, va, preferred_element_type=F32)
        out_refs[0][...] = y1
        out_refs[1][...] = y2
    else:
        out_refs[0][...] = (y1 * norm).astype(BF16)


def _seq_dft(z4, wcs, tables, B, nseg, norm):
    fw = FOURIER_WIDTH
    tm = 512
    both = nseg > 1
    u_col = IN_COLS // fw - 1
    if both:
        assert nseg == N_PLANES
        n_cls = N_PLANES
        u_spec = pl.BlockSpec((None, nseg, None, PLANE_ROWS, fw), lambda b, s, i: (b, 0, s, 0, u_col))
    else:
        n_cls = 1
        u_spec = pl.BlockSpec((None, None, N_PLANES, PLANE_ROWS, fw), lambda b, s, i: (b, 0, 0, 0, u_col))
    m_spec = pl.BlockSpec((tm, SEG), lambda b, s, i: (i, 0))
    o_spec = pl.BlockSpec((None, None, tm, fw), lambda b, s, i: (b, s, i, 0))
    odt = F32 if both else BF16
    outs = pl.pallas_call(
        functools.partial(_seq_dft_kernel, norm=norm, both=both),
        grid=(B, n_cls, SEG // tm),
        in_specs=[m_spec, m_spec, u_spec,
                  pl.BlockSpec((N_GROUPS, GROUP_DIM, 2 * GROUP_DIM), lambda b, s, i: (0, 0, 0))],
        out_specs=[o_spec] * (2 if both else 1),
        out_shape=[jax.ShapeDtypeStruct((B, n_cls, SEG, fw), odt)] * (2 if both else 1),
        scratch_shapes=[pltpu.VMEM((SEG, fw), BF16), pltpu.VMEM((SEG, fw), BF16)],
        compiler_params=_params(("parallel", "parallel", "arbitrary"), 48),
        name="four_seq_r4" if both else "four_seq_r1",
    )(tables[0], tables[1], z4, wcs)
    return outs if both else outs[0]


def _twiddle_kernel(y1_ref, y2_ref, tw_ref, f_ref, *, radix, norm):
    tw = tw_ref[...]
    acc = None
    for s in range(radix):
        term = tw[:, s:s + 1] * y1_ref[s] - tw[:, radix + s:radix + s + 1] * y2_ref[s]
        acc = term if acc is None else acc + term
    f_ref[...] = (acc * norm).astype(BF16)


def _twiddle_combine(y1, y2, norm):
    B, radix, _, fw = y1.shape
    nseg = radix
    S = nseg * SEG
    tm = 256
    g, r, m = np.meshgrid(np.arange(nseg), np.arange(N_PLANES), np.arange(PLANE_ROWS), indexing="ij")
    j = (SEG * g + N_PLANES * m + r).astype(np.float64)[..., None]
    ang = 2.0 * np.pi * ((j * np.arange(radix)) % S) / S
    tw = np.zeros((nseg, N_PLANES, PLANE_ROWS, LANES), np.float32)
    tw[..., :radix] = np.cos(ang)
    tw[..., radix:2 * radix] = np.sin(ang)
    nb = PLANE_ROWS // tm
    y_spec = pl.BlockSpec((None, radix, tm, fw), lambda b, r, i, g: (b, 0, r * nb + i, 0))
    return pl.pallas_call(
        functools.partial(_twiddle_kernel, radix=radix, norm=norm),
        grid=(B, N_PLANES, nb, nseg),
        in_specs=[y_spec, y_spec, pl.BlockSpec((None, None, tm, LANES), lambda b, r, i, g: (g, r, i, 0))],
        out_specs=pl.BlockSpec((None, None, None, tm, fw), lambda b, r, i, g: (b, g, r, i, 0)),
        out_shape=jax.ShapeDtypeStruct((B, nseg, N_PLANES, PLANE_ROWS, fw), BF16),
        compiler_params=_params(("parallel", "parallel", "parallel", "arbitrary"), 48),
        name="four_twiddle",
    )(y1, y2, jnp.asarray(tw))


def _fourier(z4s, wcs, tables_nat, tables_plane, B, nseg):
    G = B * nseg
    fw = FOURIER_WIDTH
    norm = 1.0 / math.sqrt(nseg * SEG * GROUP_DIM)
    if nseg == 1:
        f = _seq_dft(z4s, wcs, tables_plane, B, nseg, norm)
    else:
        y1, y2 = _seq_dft(z4s, wcs, tables_nat, B, nseg, norm)
        f = _twiddle_combine(y1, y2, norm)
    return f.reshape(G, N_PLANES, PLANE_ROWS, fw)


def _out_proj_kernel(o1_ref, o4_ref, o16_ref, l1_ref, l4_ref, l16_ref, f_ref, x_ref, mod_ref,
                     ga_ref, gf_ref, gpm_ref, gpl_ref, e2_ref, w_ref, x1_ref, h2_ref,
                     xslab, wcat, wbc, mixed, ybuf, *, tm):
    aw = ATTN_WIDTH
    pr = tm // N_PLANES
    for s in range(D_SLABS):
        xslab[s] = x_ref[:, _lane_slab(s)]

    for r in range(N_PLANES):
        rows = slice(r * pr, (r + 1) * pr)
        l1, l4, l16 = l1_ref[r], l4_ref[r], l16_ref[r]
        top = jnp.maximum(jnp.maximum(l1, l4), l16)
        e1, e4, e16 = jnp.exp(l1 - top), jnp.exp(l4 - top), jnp.exp(l16 - top)
        inv = 1.0 / (e1 + e4 + e16)
        for b, e in enumerate((e1, e4, e16)):
            w = e * inv
            hi = w.astype(BF16)
            wcat[b, rows, 0:LANES] = hi
            wcat[b, rows, LANES:] = (w - hi.astype(F32)).astype(BF16)
    for b in range(3):
        wbc[b] = jnp.dot(wcat[b], e2_ref[...], preferred_element_type=F32)

    for r in range(N_PLANES):
        rows = slice(r * pr, (r + 1) * pr)
        o16 = jnp.concatenate([o16_ref[r, h] for h in range(N_HEADS)], axis=1)
        a = (wbc[0, rows, :] * o1_ref[r].astype(F32) + wbc[1, rows, :] * o4_ref[r].astype(F32)
             + wbc[2, rows, :] * o16)
        mixed[rows, :aw] = _rms_rows(a, ga_ref[...]).astype(BF16)
        mixed[rows, aw:] = _rms_rows(f_ref[r].astype(F32), gf_ref[...]).astype(BF16)

    ybuf[...] = jnp.dot(mixed[...], w_ref[...], preferred_element_type=F32)

    gate1 = mod_ref[2:3, :]
    shift2 = mod_ref[3:4, :]
    scale2 = 1.0 + mod_ref[4:5, :]
    for r in range(N_PLANES):
        rows = slice(r * pr, (r + 1) * pr)
        xr = jnp.concatenate([xslab[s, pl.ds(r, pr, stride=N_PLANES), :] for s in range(D_SLABS)], axis=1)
        x1 = xr + gate1 * _rms_rows(ybuf[rows, :], gpm_ref[...])
        x1_ref[r] = x1
        h2_ref[r] = (_rms_rows(x1, gpl_ref[...]) * scale2 + shift2).astype(BF16)


def _lane_broadcast_matrix():
    e = np.zeros((2 * LANES, ATTN_WIDTH), np.float32)
    for h in range(N_HEADS):
        e[h, h * HEAD_DIM:(h + 1) * HEAD_DIM] = 1.0
        e[LANES + h, h * HEAD_DIM:(h + 1) * HEAD_DIM] = 1.0
    return jnp.asarray(e, BF16)


def _out_proj(o1, o4, o16, l1, l4, l16, f, xg, mod, nseg, g_attn_out, g_fourier_out, g_post_mix, g_pre_mlp, w_out_bf16):
    G = xg.shape[0]
    tm = 256
    pr = tm // N_PLANES
    aw, fw = ATTN_WIDTH, FOURIER_WIDTH
    plane = lambda g, i: (g, 0, i, 0)
    const2 = lambda g, i: (0, 0)
    return pl.pallas_call(
        functools.partial(_out_proj_kernel, tm=tm),
        grid=(G, SEG // tm),
        in_specs=[
            pl.BlockSpec((None, N_PLANES, pr, aw), plane),
            pl.BlockSpec((None, N_PLANES, pr, aw), plane),
            pl.BlockSpec((None, N_PLANES, N_HEADS, pr, LANES), lambda g, i: (g, 0, 0, i, 0)),
            pl.BlockSpec((None, N_PLANES, pr, LANES), plane),
            pl.BlockSpec((None, N_PLANES, pr, LANES), plane),
            pl.BlockSpec((None, N_PLANES, pr, LANES), plane),
            pl.BlockSpec((None, N_PLANES, pr, fw), plane),
            pl.BlockSpec((None, tm, D_MODEL), lambda g, i: (g, i, 0)),
            pl.BlockSpec((None, N_MOD, D_MODEL), lambda g, i: (g // nseg, 0, 0)),
            pl.BlockSpec((1, aw), const2),
            pl.BlockSpec((1, fw), const2),
            pl.BlockSpec((1, D_MODEL), const2),
            pl.BlockSpec((1, D_MODEL), const2),
            pl.BlockSpec((2 * LANES, aw), const2),
            pl.BlockSpec((aw + fw, D_MODEL), const2),
        ],
        out_specs=[pl.BlockSpec((None, N_PLANES, pr, D_MODEL), plane)] * 2,
        out_shape=[jax.ShapeDtypeStruct((G, N_PLANES, PLANE_ROWS, D_MODEL), F32),
                   jax.ShapeDtypeStruct((G, N_PLANES, PLANE_ROWS, D_MODEL), BF16)],
        scratch_shapes=[
            pltpu.VMEM((D_SLABS, tm, LANES), F32),
            pltpu.VMEM((3, tm, 2 * LANES), BF16),
            pltpu.VMEM((3, tm, aw), F32),
            pltpu.VMEM((tm, aw + fw), BF16),
            pltpu.VMEM((tm, D_MODEL), F32),
        ],
        compiler_params=_params(("parallel", "parallel"), 56),
        name="out_proj",
    )(o1, o4, o16, l1, l4, l16, f, xg, mod, g_attn_out.reshape(1, aw), g_fourier_out.reshape(1, fw),
      g_post_mix.reshape(1, D_MODEL), g_pre_mlp.reshape(1, D_MODEL), _lane_broadcast_matrix(), w_out_bf16)


def _mlp_kernel(h_ref, x1_ref, mod_ref, g_ref, w1_ref, w2_ref, out_ref, acc_ref, oslab, *, tm, n_j):
    j = pl.program_id(2)
    pr = tm // N_PLANES

    @pl.when(j == 0)
    def _():
        acc_ref[...] = jnp.zeros_like(acc_ref)

    h = h_ref[...].reshape(tm, D_MODEL)
    hid = jnp.dot(h, w1_ref[...], preferred_element_type=F32)
    hid = jnp.square(jnp.maximum(hid, 0.0)).astype(BF16)
    acc_ref[...] += jnp.dot(hid, w2_ref[...], preferred_element_type=F32)

    @pl.when(j == n_j - 1)
    def _():
        gate2 = mod_ref[5:6, :]
        rc = 64
        for r in range(N_PLANES):
            def body(c, carry):
                m0 = pl.multiple_of(c * rc, rc)
                y = acc_ref[pl.ds(r * pr + m0, rc), :]
                v = x1_ref[r, pl.ds(m0, rc), :] + gate2 * _rms_rows(y, g_ref[...])
                for s in range(D_SLABS):
                    oslab[s, pl.ds(r + N_PLANES * m0, rc, stride=N_PLANES), :] = v[:, _lane_slab(s)]
                return carry

            lax.fori_loop(0, pr // rc, body, 0)
        for s in range(D_SLABS):
            out_ref[:, _lane_slab(s)] = oslab[s]


def _mlp(h2, x1, mod, nseg, g_post_mlp, w1_bf16, w2_bf16):
    G = x1.shape[0]
    tm, tf = 512, 1024
    pr = tm // N_PLANES
    n_j = D_FF // tf
    plane = lambda g, i, j: (g, 0, i, 0)
    return pl.pallas_call(
        functools.partial(_mlp_kernel, tm=tm, n_j=n_j),
        grid=(G, SEG // tm, n_j),
        in_specs=[
            pl.BlockSpec((None, N_PLANES, pr, D_MODEL), plane),
            pl.BlockSpec((None, N_PLANES, pr, D_MODEL), plane),
            pl.BlockSpec((None, N_MOD, D_MODEL), lambda g, i, j: (g // nseg, 0, 0)),
            pl.BlockSpec((1, D_MODEL), lambda g, i, j: (0, 0)),
            pl.BlockSpec((D_MODEL, tf), lambda g, i, j: (0, j)),
            pl.BlockSpec((tf, D_MODEL), lambda g, i, j: (j, 0)),
        ],
        out_specs=pl.BlockSpec((None, tm, D_MODEL), lambda g, i, j: (g, i, 0)),
        out_shape=jax.ShapeDtypeStruct((G, SEG, D_MODEL), F32),
        scratch_shapes=[pltpu.VMEM((tm, D_MODEL), F32), pltpu.VMEM((D_SLABS, tm, LANES), F32)],
        compiler_params=_params(("parallel", "parallel", "arbitrary"), 56),
        name="mlp",
    )(h2, x1, mod, g_post_mlp.reshape(1, D_MODEL), w1_bf16, w2_bf16)


def _layer(x, mod, p):
    B, S, _ = x.shape
    assert S % SEG == 0
    nseg = S // SEG
    G = B * nseg
    xg = x.reshape(G, SEG, D_MODEL)
    z4, z16 = _in_proj(xg, mod, nseg, p["g_pre_mix"], p["w_in"])
    z4s = z4.reshape(B, nseg, N_PLANES, PLANE_ROWS, IN_COLS)
    z16s = z16.reshape(B, nseg, 4, N_PLANES, CLASS_ROWS, 3 * ATTN_WIDTH)
    o1, l1 = _attn_d1(z4s, B, nseg)
    o4, l4 = _attn_d4(z4s, B, nseg)
    o16, l16 = _attn_d16(z16s, B, nseg)
    f = _fourier(z4s, p["wcs"], p["tables_nat"], p["tables_plane"], B, nseg)
    pshape = (G, N_PLANES, PLANE_ROWS)
    x1, h2 = _out_proj(
        o1.reshape(pshape + (ATTN_WIDTH,)), o4.reshape(pshape + (ATTN_WIDTH,)),
        o16.reshape((G, N_PLANES, N_HEADS, PLANE_ROWS, LANES)),
        l1.reshape(pshape + (LANES,)), l4.reshape(pshape + (LANES,)), l16.reshape(pshape + (LANES,)),
        f, xg, mod, nseg, p["g_attn_out"], p["g_fourier_out"], p["g_post_mix"], p["g_pre_mlp"], p["w_out"])
    out = _mlp(h2, x1, mod, nseg, p["g_post_mlp"], p["w_mlp_in"], p["w_mlp_out"])
    return out.reshape(B, S, D_MODEL)


def kernel(x_prompt, x_sample, c_prompt, c_sample, w_ada, b_ada, g_pre_mix, w_in, g_attn_out, w_fourier,
           g_fourier_out, w_out, g_post_mix, g_pre_mlp, w_mlp_in, w_mlp_out, g_post_mlp):
    depth = w_ada.shape[0]
    nb_p, nb_s = c_prompt.shape[0], c_sample.shape[0]
    pad = (-(nb_p + nb_s)) % 8
    c_all = jnp.concatenate([c_prompt, c_sample, jnp.zeros((pad, D_MODEL), F32)], axis=0)
    tables_nat, tables_plane = _dft_tables()
    xp, xs = x_prompt, x_sample
    for l in range(depth):
        mod = _modulation(c_all, w_ada[l], b_ada[l])
        mod_p = mod[:nb_p].reshape(nb_p, N_MOD, D_MODEL)
        mod_s = mod[nb_p:nb_p + nb_s].reshape(nb_s, N_MOD, D_MODEL)
        p = dict(
            g_pre_mix=g_pre_mix[l], w_in=w_in[l].astype(BF16), g_attn_out=g_attn_out[l],
            g_fourier_out=g_fourier_out[l], w_out=w_out[l].astype(BF16), g_post_mix=g_post_mix[l],
            g_pre_mlp=g_pre_mlp[l], w_mlp_in=w_mlp_in[l].astype(BF16), w_mlp_out=w_mlp_out[l].astype(BF16),
            g_post_mlp=g_post_mlp[l], wcs=_fold_channel_dft(w_fourier[l]),
            tables_nat=tables_nat, tables_plane=tables_plane,
        )
        xp = _layer(xp, mod_p, p)
        xs = _layer(xs, mod_s, p)
    return (xp, xs)
```

```python
import functools
import math

import numpy as np
import jax
import jax.numpy as jnp
from jax import lax
from jax.experimental import pallas as pl
from jax.experimental.pallas import tpu as pltpu

D_MODEL = 2048
ATTN_WIDTH = 1024
FOURIER_WIDTH = 1024
HEAD_DIM = 128
N_HEADS = 8
N_GROUPS = 4
GROUP_DIM = 256
D_FF = 8192
IN_COLS = 3 * ATTN_WIDTH + FOURIER_WIDTH
DILATED_BRANCHES = ((128, 1), (512, 4), (2048, 16))
HALF = 64
N_MOD = 6
RMS_EPS = 1e-6
NEG_INF = -1e30

SEG = 2048
N_PLANES = 4
PLANE_ROWS = SEG // N_PLANES
N_CLASSES = 16
CLASS_ROWS = SEG // N_CLASSES
LANES = 128
D_SLABS = D_MODEL // LANES
Q_SUB = 128
K_SUB = 256
K_PAD = K_SUB - (Q_SUB + 2 * HALF)

F32 = jnp.float32
BF16 = jnp.bfloat16
MIB = 1024 * 1024

assert DILATED_BRANCHES == ((128, 1), (512, 4), (2048, 16))
assert all(w // (2 * d) == HALF for w, d in DILATED_BRANCHES)


def _params(semantics, vmem_mib):
    return pltpu.CompilerParams(dimension_semantics=semantics, vmem_limit_bytes=vmem_mib * MIB)


def _rms_rows(v, gain):
    ms = jnp.mean(v * v, axis=-1, keepdims=True)
    return v * lax.rsqrt(ms + RMS_EPS) * gain


def _lane_slab(s):
    return slice(s * LANES, (s + 1) * LANES)


def _mod_kernel(c_ref, w_ref, b_ref, o_ref):
    c = c_ref[...]
    a = (c * jax.nn.sigmoid(c)).astype(BF16)
    o_ref[...] = jnp.dot(a, w_ref[...].astype(BF16), preferred_element_type=F32) + b_ref[...]


def _modulation(c_all, w_ada, b_ada):
    rows = c_all.shape[0]
    n = w_ada.shape[1]
    tn = 1024
    return pl.pallas_call(
        _mod_kernel,
        grid=(n // tn,),
        in_specs=[
            pl.BlockSpec((rows, D_MODEL), lambda j: (0, 0)),
            pl.BlockSpec((D_MODEL, tn), lambda j: (0, j)),
            pl.BlockSpec((1, tn), lambda j: (0, j)),
        ],
        out_specs=pl.BlockSpec((rows, tn), lambda j: (0, j)),
        out_shape=jax.ShapeDtypeStruct((rows, n), F32),
        compiler_params=_params(("parallel",), 40),
        name="mod",
    )(c_all, w_ada, b_ada.reshape(1, n))


def _in_proj_kernel(x_ref, mod_ref, g_ref, w_ref, z4_ref, z16_ref, h_ref, hslab, aslab, *, tm, tn, nb):
    j = pl.program_id(2)
    pr = tm // N_PLANES
    pb = nb // N_PLANES

    @pl.when(j == 0)
    def _():
        shift = mod_ref[0:1, :]
        scale1 = 1.0 + mod_ref[1:2, :]
        gain = g_ref[...]

        def block(c, carry):
            n0 = pl.multiple_of(c * nb, nb)
            rq = 16
            for q in range(nb // rq):
                h = _rms_rows(x_ref[pl.ds(n0 + q * rq, rq), :], gain) * scale1 + shift
                for s in range(D_SLABS):
                    hslab[s, q * rq:(q + 1) * rq, :] = h[:, _lane_slab(s)]
            m0 = pl.multiple_of(c * pb, pb)
            for r in range(N_PLANES):
                for s in range(D_SLABS):
                    h_ref[pl.ds(r * pr + m0, pb), _lane_slab(s)] = (
                        hslab[s, pl.ds(r, pb, stride=N_PLANES), :].astype(BF16))
            return carry

        lax.fori_loop(0, tm // nb, block, 0)

    acc = jnp.dot(h_ref[...], w_ref[...], preferred_element_type=F32)
    acc = acc * jnp.where(j == 1, HEAD_DIM ** -0.5, 1.0).astype(F32)
    for r in range(N_PLANES):
        z4_ref[r] = acc[r * pr:(r + 1) * pr].astype(BF16)
    for s in range(tn // LANES):
        aslab[s] = acc[:, _lane_slab(s)]
    cr = pr // 4
    for r in range(N_PLANES):
        for a in range(4):
            for s in range(tn // LANES):
                z16_ref[r + 4 * a, :, _lane_slab(s)] = (
                    aslab[s, pl.ds(r * pr + a, cr, stride=4), :].astype(BF16))


def _in_proj(xg, mod, nseg, g_pre_mix, w_in_bf16):
    G = xg.shape[0]
    tm, tn = 1024, 1024
    n_j = IN_COLS // tn
    return pl.pallas_call(
        functools.partial(_in_proj_kernel, tm=tm, tn=tn, nb=256),
        grid=(G, SEG // tm, n_j),
        in_specs=[
            pl.BlockSpec((None, tm, D_MODEL), lambda g, i, j: (g, i, 0)),
            pl.BlockSpec((None, N_MOD, D_MODEL), lambda g, i, j: (g // nseg, 0, 0)),
            pl.BlockSpec((1, D_MODEL), lambda g, i, j: (0, 0)),
            pl.BlockSpec((D_MODEL, tn), lambda g, i, j: (0, (j + n_j - 1) % n_j)),
        ],
        out_specs=[
            pl.BlockSpec((None, N_PLANES, tm // N_PLANES, tn), lambda g, i, j: (g, 0, i, (j + n_j - 1) % n_j)),
            pl.BlockSpec((None, N_CLASSES, tm // N_CLASSES, tn), lambda g, i, j: (g, 0, i, jnp.maximum(j, 1) - 1)),
        ],
        out_shape=[
            jax.ShapeDtypeStruct((G, N_PLANES, PLANE_ROWS, IN_COLS), BF16),
            jax.ShapeDtypeStruct((G, N_CLASSES, CLASS_ROWS, 3 * ATTN_WIDTH), BF16),
        ],
        scratch_shapes=[
            pltpu.VMEM((tm, D_MODEL), BF16),
            pltpu.VMEM((D_SLABS, 256, LANES), F32),
            pltpu.VMEM((tn // LANES, tm, LANES), F32),
        ],
        compiler_params=_params(("parallel", "parallel", "arbitrary"), 56),
        name="in_proj",
    )(xg, mod, g_pre_mix.reshape(1, D_MODEL), w_in_bf16)


def _softmax_pv(s, vh):
    m = jnp.max(s, axis=-1, keepdims=True)
    p = jnp.exp(s - m)
    den = jnp.sum(p, axis=-1, keepdims=True)
    o = jnp.dot(p.astype(BF16), vh, preferred_element_type=F32) / den
    return o, m + jnp.log(den)


def _scores(qh, kh, bias):
    return lax.dot_general(qh, kh, (((1,), (1,)), ((), ())), preferred_element_type=F32) + bias


def _alibi_tile(rel, dil):
    slopes = 2.0 ** (-8.0 * (np.arange(N_HEADS, dtype=np.float64) + 1.0) / N_HEADS)
    bias = -(slopes * dil)[:, None, None] * rel[None]
    return jnp.asarray(np.where(rel[None] <= HALF, bias, NEG_INF), dtype=F32)


def _band_bias(dil):
    rel = np.abs((np.arange(K_SUB)[None, :] - HALF) - np.arange(Q_SUB)[:, None]).astype(np.float64)
    return _alibi_tile(rel, dil)


def _plane_bias():
    qr, qm = np.divmod(np.arange(Q_SUB), Q_SUB // N_PLANES)
    kr, km = np.divmod(np.arange(K_SUB), K_SUB // N_PLANES)
    rel = np.abs(N_PLANES * (km[None, :] - HALF // N_PLANES - qm[:, None]) + (kr[None, :] - qr[:, None])).astype(np.float64)
    return _alibi_tile(rel, 1)


def _attn_class_kernel(q_ref, kp_ref, kc_ref, kn_ref, vp_ref, vc_ref, vn_ref, bias_ref,
                       o_ref, lse_ref, kext, vext, *, lq, n_i):
    i = pl.program_id(1)
    n_sub = lq // Q_SUB
    col = lax.broadcasted_iota(jnp.int32, (Q_SUB, K_SUB), 1)
    lane = lax.broadcasted_iota(jnp.int32, (Q_SUB, LANES), 1)

    if K_PAD:
        kext[2 * HALF + lq:, :] = jnp.zeros((K_PAD, ATTN_WIDTH), BF16)
        vext[2 * HALF + lq:, :] = jnp.zeros((K_PAD, ATTN_WIDTH), BF16)

    def plane(r, carry):
        kext[0:HALF, :] = kp_ref[r]
        kext[HALF:HALF + lq, :] = kc_ref[r]
        kext[HALF + lq:2 * HALF + lq, :] = kn_ref[r]
        vext[0:HALF, :] = vp_ref[r]
        vext[HALF:HALF + lq, :] = vc_ref[r]
        vext[HALF + lq:2 * HALF + lq, :] = vn_ref[r]
        for sb in range(n_sub):
            r0 = sb * Q_SUB
            lse_tile = jnp.zeros((Q_SUB, LANES), F32)
            for h in range(N_HEADS):
                hc = _lane_slab(h)
                s = _scores(q_ref[r, r0:r0 + Q_SUB, hc], kext[r0:r0 + K_SUB, hc], bias_ref[h])
                if sb == 0:
                    s = jnp.where((col >= HALF) | (i > 0), s, NEG_INF)
                if sb == n_sub - 1:
                    s = jnp.where((col < Q_SUB + HALF) | (i < n_i - 1), s, NEG_INF)
                o, lse = _softmax_pv(s, vext[r0:r0 + K_SUB, hc])
                o_ref[r, r0:r0 + Q_SUB, hc] = o.astype(BF16)
                lse_tile = jnp.where(lane == h, lse, lse_tile)
            lse_ref[r, r0:r0 + Q_SUB, :] = lse_tile
        return carry

    lax.fori_loop(0, N_PLANES, plane, 0)


def _attn_d4(z4, B, nseg):
    aw = ATTN_WIDTH
    lq = PLANE_ROWS
    hb = lq // HALF

    def cur(part):
        return pl.BlockSpec((None, None, N_PLANES, lq, aw), lambda b, i: (b, i, 0, 0, part))

    def prev(part):
        return pl.BlockSpec((None, None, N_PLANES, HALF, aw), lambda b, i: (b, jnp.maximum(i - 1, 0), 0, hb - 1, part))

    def nxt(part):
        return pl.BlockSpec((None, None, N_PLANES, HALF, aw), lambda b, i: (b, jnp.minimum(i + 1, nseg - 1), 0, 0, part))

    return pl.pallas_call(
        functools.partial(_attn_class_kernel, lq=lq, n_i=nseg),
        grid=(B, nseg),
        in_specs=[cur(0), prev(1), cur(1), nxt(1), prev(2), cur(2), nxt(2),
                  pl.BlockSpec((N_HEADS, Q_SUB, K_SUB), lambda b, i: (0, 0, 0))],
        out_specs=[
            pl.BlockSpec((None, None, N_PLANES, lq, aw), lambda b, i: (b, i, 0, 0, 0)),
            pl.BlockSpec((None, None, N_PLANES, lq, LANES), lambda b, i: (b, i, 0, 0, 0)),
        ],
        out_shape=[
            jax.ShapeDtypeStruct((B, nseg, N_PLANES, PLANE_ROWS, aw), BF16),
            jax.ShapeDtypeStruct((B, nseg, N_PLANES, PLANE_ROWS, LANES), F32),
        ],
        scratch_shapes=[pltpu.VMEM((lq + 2 * HALF + K_PAD, aw), BF16)] * 2,
        compiler_params=_params(("parallel", "arbitrary"), 52),
        name="attn_d4",
    )(z4, z4, z4, z4, z4, z4, z4, _band_bias(4))


def _attn_d16_kernel(q_ref, kp_ref, kc_ref, kn_ref, vp_ref, vc_ref, vn_ref, bias_ref, o_ref, lse_ref, *, n_i, n_pl):
    i = pl.program_id(2)
    col = lax.broadcasted_iota(jnp.int32, (Q_SUB, K_SUB), 1)
    lane = lax.broadcasted_iota(jnp.int32, (Q_SUB, LANES), 1)

    n_sub = CLASS_ROWS // Q_SUB
    pad = [jnp.zeros((K_PAD, HEAD_DIM), BF16)] if K_PAD else []

    def plane(r, carry):
        for a in range(4):
            for sb in range(n_sub):
                r0 = sb * Q_SUB
                lse_tile = jnp.zeros((Q_SUB, LANES), F32)
                for h in range(N_HEADS):
                    hc = _lane_slab(h)
                    kh = jnp.concatenate([kp_ref[a, r, :, hc], kc_ref[a, r, :, hc], kn_ref[a, r, :, hc]] + pad,
                                         axis=0)[r0:r0 + K_SUB]
                    vh = jnp.concatenate([vp_ref[a, r, :, hc], vc_ref[a, r, :, hc], vn_ref[a, r, :, hc]] + pad,
                                         axis=0)[r0:r0 + K_SUB]
                    s = _scores(q_ref[a, r, r0:r0 + Q_SUB, hc], kh, bias_ref[h])
                    if sb == 0:
                        s = jnp.where((col >= HALF) | (i > 0), s, NEG_INF)
                    if sb == n_sub - 1:
                        s = jnp.where((col < Q_SUB + HALF) | (i < n_i - 1), s, NEG_INF)
                    o, lse = _softmax_pv(s, vh)
                    o_ref[r, h, pl.ds(a + 4 * r0, Q_SUB, stride=4), :] = o
                    lse_tile = jnp.where(lane == h, lse, lse_tile)
                lse_ref[r, pl.ds(a + 4 * r0, Q_SUB, stride=4), :] = lse_tile
        return carry

    lax.fori_loop(0, n_pl, plane, 0)


def _attn_d16(z16, B, nseg):
    aw = ATTN_WIDTH
    lq = CLASS_ROWS
    hb = lq // HALF
    n_pl = 2

    def cur(part):
        return pl.BlockSpec((None, None, 4, n_pl, lq, aw), lambda b, r, i: (b, i, 0, r, 0, part))

    def prev(part):
        return pl.BlockSpec((None, None, 4, n_pl, HALF, aw),
                            lambda b, r, i: (b, jnp.maximum(i - 1, 0), 0, r, hb - 1, part))

    def nxt(part):
        return pl.BlockSpec((None, None, 4, n_pl, HALF, aw),
                            lambda b, r, i: (b, jnp.minimum(i + 1, nseg - 1), 0, r, 0, part))

    return pl.pallas_call(
        functools.partial(_attn_d16_kernel, n_i=nseg, n_pl=n_pl),
        grid=(B, N_PLANES // n_pl, nseg),
        in_specs=[cur(0), prev(1), cur(1), nxt(1), prev(2), cur(2), nxt(2),
                  pl.BlockSpec((N_HEADS, Q_SUB, K_SUB), lambda b, r, i: (0, 0, 0))],
        out_specs=[
            pl.BlockSpec((None, None, n_pl, N_HEADS, PLANE_ROWS, LANES), lambda b, r, i: (b, i, r, 0, 0, 0)),
            pl.BlockSpec((None, None, n_pl, PLANE_ROWS, LANES), lambda b, r, i: (b, i, r, 0, 0)),
        ],
        out_shape=[
            jax.ShapeDtypeStruct((B, nseg, N_PLANES, N_HEADS, PLANE_ROWS, LANES), F32),
            jax.ShapeDtypeStruct((B, nseg, N_PLANES, PLANE_ROWS, LANES), F32),
        ],
        compiler_params=_params(("parallel", "parallel", "arbitrary"), 40),
        name="attn_d16",
    )(z16, z16, z16, z16, z16, z16, z16, _band_bias(16))


def _attn_d1_kernel(q_ref, kp_ref, kc_ref, kn_ref, vp_ref, vc_ref, vn_ref, bias_ref,
                    o_ref, lse_ref, kext, vext, *, mq, n_t):
    t = pl.program_id(1)
    qm = Q_SUB // N_PLANES
    km = K_SUB // N_PLANES
    hm = HALF // N_PLANES
    pm = K_PAD // N_PLANES
    for r in range(N_PLANES):
        kext[r, 0:hm, :] = kp_ref[r]
        kext[r, hm:hm + mq, :] = kc_ref[r]
        kext[r, hm + mq:2 * hm + mq, :] = kn_ref[r]
        vext[r, 0:hm, :] = vp_ref[r]
        vext[r, hm:hm + mq, :] = vc_ref[r]
        vext[r, hm + mq:2 * hm + mq, :] = vn_ref[r]
        if pm:
            kext[r, 2 * hm + mq:, :] = jnp.zeros((pm, ATTN_WIDTH), BF16)
            vext[r, 2 * hm + mq:, :] = jnp.zeros((pm, ATTN_WIDTH), BF16)

    n_sub = mq // qm
    colm = lax.broadcasted_iota(jnp.int32, (Q_SUB, K_SUB), 1) & (km - 1)
    lane = lax.broadcasted_iota(jnp.int32, (Q_SUB, LANES), 1)
    for sb in range(n_sub):
        m1 = sb * qm
        lse_tile = jnp.zeros((Q_SUB, LANES), F32)
        for h in range(N_HEADS):
            hc = _lane_slab(h)
            qh = jnp.concatenate([q_ref[r, m1:m1 + qm, hc] for r in range(N_PLANES)], axis=0)
            kh = jnp.concatenate([kext[r, m1:m1 + km, hc] for r in range(N_PLANES)], axis=0)
            vh = jnp.concatenate([vext[r, m1:m1 + km, hc] for r in range(N_PLANES)], axis=0)
            s = _scores(qh, kh, bias_ref[h])
            if sb == 0:
                s = jnp.where((colm >= hm) | (t > 0), s, NEG_INF)
            if sb == n_sub - 1:
                s = jnp.where((colm < qm + hm) | (t < n_t - 1), s, NEG_INF)
            o, lse = _softmax_pv(s, vh)
            for r in range(N_PLANES):
                o_ref[r, m1:m1 + qm, hc] = o[r * qm:(r + 1) * qm].astype(BF16)
            lse_tile = jnp.where(lane == h, lse, lse_tile)
        for r in range(N_PLANES):
            lse_ref[r, m1:m1 + qm, :] = lse_tile[r * qm:(r + 1) * qm]


def _attn_d1(z4, B, nseg):
    aw = ATTN_WIDTH
    mq = 256
    hm = HALF // N_PLANES
    per_seg = PLANE_ROWS // mq
    hb_step = mq // hm
    hb_seg = PLANE_ROWS // hm
    n_t = nseg * per_seg

    def cur(part):
        return pl.BlockSpec((None, None, N_PLANES, mq, aw), lambda b, t: (b, t // per_seg, 0, t % per_seg, part))

    def prev(part):
        def imap(b, t):
            q = jnp.maximum(t * hb_step - 1, 0)
            return (b, q // hb_seg, 0, q % hb_seg, part)
        return pl.BlockSpec((None, None, N_PLANES, hm, aw), imap)

    def nxt(part):
        def imap(b, t):
            q = jnp.minimum((t + 1) * hb_step, nseg * hb_seg - 1)
            return (b, q // hb_seg, 0, q % hb_seg, part)
        return pl.BlockSpec((None, None, N_PLANES, hm, aw), imap)

    return pl.pallas_call(
        functools.partial(_attn_d1_kernel, mq=mq, n_t=n_t),
        grid=(B, n_t),
        in_specs=[cur(0), prev(1), cur(1), nxt(1), prev(2), cur(2), nxt(2),
                  pl.BlockSpec((N_HEADS, Q_SUB, K_SUB), lambda b, t: (0, 0, 0))],
        out_specs=[
            pl.BlockSpec((None, None, N_PLANES, mq, aw), lambda b, t: (b, t // per_seg, 0, t % per_seg, 0)),
            pl.BlockSpec((None, None, N_PLANES, mq, LANES), lambda b, t: (b, t // per_seg, 0, t % per_seg, 0)),
        ],
        out_shape=[
            jax.ShapeDtypeStruct((B, nseg, N_PLANES, PLANE_ROWS, aw), BF16),
            jax.ShapeDtypeStruct((B, nseg, N_PLANES, PLANE_ROWS, LANES), F32),
        ],
        scratch_shapes=[pltpu.VMEM((N_PLANES, mq + 2 * hm + K_PAD // N_PLANES, aw), BF16)] * 2,
        compiler_params=_params(("parallel", "arbitrary"), 32),
        name="attn_d1",
    )(z4, z4, z4, z4, z4, z4, z4, _plane_bias())


def _dft_tables():
    n = SEG
    n1 = 32
    p_per = n1 // N_PLANES
    cols_nat = np.arange(n, dtype=np.float64)
    cols_plane = (N_PLANES * (np.arange(n) % PLANE_ROWS) + np.arange(n) // PLANE_ROWS).astype(np.float64)

    def build(cols):
        a = 2.0 * np.pi * ((np.arange(n // n1)[:, None] * n1 * cols[None, :]) % n) / n
        jr = (N_PLANES * np.arange(p_per)[None, :] + np.arange(N_PLANES)[:, None]).astype(np.float64)
        b = 2.0 * np.pi * ((jr[:, :, None] * cols[None, None, :]) % n) / n
        ca, sa = jnp.asarray(np.cos(a), F32)[None, :, None, :], jnp.asarray(np.sin(a), F32)[None, :, None, :]
        cb, sb = jnp.asarray(np.cos(b), F32)[:, None, :, :], jnp.asarray(np.sin(b), F32)[:, None, :, :]
        cos = (ca * cb - sa * sb).reshape(n, n).astype(BF16)
        msin = (-(sa * cb + ca * sb)).reshape(n, n).astype(BF16)
        return cos, msin

    return build(cols_nat), build(cols_plane)


def _fold_kernel(c_ref, s_ref, w_ref, o_ref):
    w = w_ref[...]
    o_ref[:, :GROUP_DIM] = jnp.dot(c_ref[...], w, preferred_element_type=F32,
                                   precision=lax.Precision.HIGHEST).astype(BF16)
    o_ref[:, GROUP_DIM:] = jnp.dot(s_ref[...], w, preferred_element_type=F32,
                                   precision=lax.Precision.HIGHEST).astype(BF16)


def _fold_channel_dft(w_fourier):
    n = GROUP_DIM
    idx = (np.arange(n)[:, None] * np.arange(n)[None, :]) % n
    ang = 2.0 * np.pi * idx / n
    cos_c = jnp.asarray(np.cos(ang), F32)
    sin_c = jnp.asarray(np.sin(ang), F32)
    return pl.pallas_call(
        _fold_kernel,
        grid=(N_GROUPS,),
        in_specs=[
            pl.BlockSpec((n, n), lambda g: (0, 0)),
            pl.BlockSpec((n, n), lambda g: (0, 0)),
            pl.BlockSpec((None, n, n), lambda g: (g, 0, 0)),
        ],
        out_specs=pl.BlockSpec((None, n, 2 * n), lambda g: (g, 0, 0)),
        out_shape=jax.ShapeDtypeStruct((N_GROUPS, n, 2 * n), BF16),
        compiler_params=_params(("parallel",), 16),
        name="fold_channel_dft",
    )(cos_c, sin_c, w_fourier)


def _seq_dft_kernel(c_ref, ms_ref, u_ref, w_ref, *refs, norm, both):
    n_out = 2 if both else 1
    out_refs, (va_ref, vb_ref) = refs[:n_out], refs[n_out:]
    i = pl.program_id(2)

    @pl.when(i == 0)
    def _():
        for blk in range(SEG // PLANE_ROWS):
            rows = slice(blk * PLANE_ROWS, (blk + 1) * PLANE_ROWS)
            for g in range(N_GROUPS):
                cols = slice(g * GROUP_DIM, (g + 1) * GROUP_DIM)
                v = jnp.dot(u_ref[blk, :, cols], w_ref[g], preferred_element_type=F32)
                va_ref[rows, cols] = v[:, :GROUP_DIM].astype(BF16)
                vb_ref[rows, cols] = v[:, GROUP_DIM:].astype(BF16)

    c = c_ref[...]
    ms = ms_ref[...]
    va = va_ref[...]
    vb = vb_ref[...]
    y1 = jnp.dot(c, va, preferred_element_type=F32) + jnp.dot(ms, vb, preferred_element_type=F32)
    if both:
        y2 = jnp.dot(c, vb, preferred_element_type=F32) - jnp.dot(ms, va, preferred_element_type=F32)
        out_refs[0][...] = y1
        out_refs[1][...] = y2
    else:
        out_refs[0][...] = (y1 * norm).astype(BF16)


def _seq_dft(z4, wcs, tables, B, nseg, norm):
    fw = FOURIER_WIDTH
    both = nseg > 1
    u_col = IN_COLS // fw - 1
    tm = SEG // 4 if both else SEG // 2
    if both:
        assert nseg == N_PLANES
        n_cls = N_PLANES
        u_spec = pl.BlockSpec((None, nseg, None, PLANE_ROWS, fw), lambda b, s, i: (b, 0, s, 0, u_col))
    else:
        n_cls = 1
        u_spec = pl.BlockSpec((None, None, N_PLANES, PLANE_ROWS, fw), lambda b, s, i: (b, 0, 0, 0, u_col))
    m_spec = pl.BlockSpec((tm, SEG), lambda b, s, i: (i, 0))
    o_spec = pl.BlockSpec((None, None, tm, fw), lambda b, s, i: (b, s, i, 0))
    odt = F32 if both else BF16
    outs = pl.pallas_call(
        functools.partial(_seq_dft_kernel, norm=norm, both=both),
        grid=(B, n_cls, SEG // tm),
        in_specs=[m_spec, m_spec, u_spec,
                  pl.BlockSpec((N_GROUPS, GROUP_DIM, 2 * GROUP_DIM), lambda b, s, i: (0, 0, 0))],
        out_specs=[o_spec] * (2 if both else 1),
        out_shape=[jax.ShapeDtypeStruct((B, n_cls, SEG, fw), odt)] * (2 if both else 1),
        scratch_shapes=[pltpu.VMEM((SEG, fw), BF16), pltpu.VMEM((SEG, fw), BF16)],
        compiler_params=_params(("parallel", "parallel", "arbitrary"), 56),
        name="four_seq_r4" if both else "four_seq_r1",
    )(tables[0], tables[1], z4, wcs)
    return outs if both else outs[0]


def _twiddle_kernel(y1_ref, y2_ref, tw_ref, f_ref, *, radix, norm):
    tw = tw_ref[...]
    acc = None
    for s in range(radix):
        term = tw[:, s:s + 1] * y1_ref[s] - tw[:, radix + s:radix + s + 1] * y2_ref[s]
        acc = term if acc is None else acc + term
    f_ref[...] = (acc * norm).astype(BF16)


def _twiddle_combine(y1, y2, norm):
    B, radix, _, fw = y1.shape
    nseg = radix
    S = nseg * SEG
    tm = 512
    g, r, m = np.meshgrid(np.arange(nseg), np.arange(N_PLANES), np.arange(PLANE_ROWS), indexing="ij")
    j = (SEG * g + N_PLANES * m + r).astype(np.float64)[..., None]
    ang = 2.0 * np.pi * ((j * np.arange(radix)) % S) / S
    tw = np.zeros((nseg, N_PLANES, PLANE_ROWS, LANES), np.float32)
    tw[..., :radix] = np.cos(ang)
    tw[..., radix:2 * radix] = np.sin(ang)
    nb = PLANE_ROWS // tm
    y_spec = pl.BlockSpec((None, radix, tm, fw), lambda b, r, i, g: (b, 0, r * nb + i, 0))
    return pl.pallas_call(
        functools.partial(_twiddle_kernel, radix=radix, norm=norm),
        grid=(B, N_PLANES, nb, nseg),
        in_specs=[y_spec, y_spec, pl.BlockSpec((None, None, tm, LANES), lambda b, r, i, g: (g, r, i, 0))],
        out_specs=pl.BlockSpec((None, None, None, tm, fw), lambda b, r, i, g: (b, g, r, i, 0)),
        out_shape=jax.ShapeDtypeStruct((B, nseg, N_PLANES, PLANE_ROWS, fw), BF16),
        compiler_params=_params(("parallel", "parallel", "parallel", "arbitrary"), 48),
        name="four_twiddle",
    )(y1, y2, jnp.asarray(tw))


def _fourier(z4s, wcs, tables_nat, tables_plane, B, nseg):
    G = B * nseg
    fw = FOURIER_WIDTH
    norm = 1.0 / math.sqrt(nseg * SEG * GROUP_DIM)
    if nseg == 1:
        f = _seq_dft(z4s, wcs, tables_plane, B, nseg, norm)
    else:
        y1, y2 = _seq_dft(z4s, wcs, tables_nat, B, nseg, norm)
        f = _twiddle_combine(y1, y2, norm)
    return f.reshape(G, N_PLANES, PLANE_ROWS, fw)


def _out_proj_kernel(o1_ref, o4_ref, o16_ref, l1_ref, l4_ref, l16_ref, f_ref, x_ref, mod_ref,
                     ga_ref, gf_ref, gpm_ref, gpl_ref, e2_ref, w_ref, x1_ref, h2_ref,
                     xslab, wcat, wbc, mixed, ybuf, *, tm):
    aw = ATTN_WIDTH
    pr = tm // N_PLANES
    for s in range(D_SLABS):
        xslab[s] = x_ref[:, _lane_slab(s)]

    for r in range(N_PLANES):
        rows = slice(r * pr, (r + 1) * pr)
        l1, l4, l16 = l1_ref[r], l4_ref[r], l16_ref[r]
        top = jnp.maximum(jnp.maximum(l1, l4), l16)
        e1, e4, e16 = jnp.exp(l1 - top), jnp.exp(l4 - top), jnp.exp(l16 - top)
        inv = 1.0 / (e1 + e4 + e16)
        for b, e in enumerate((e1, e4, e16)):
            w = e * inv
            hi = w.astype(BF16)
            wcat[b, rows, 0:LANES] = hi
            wcat[b, rows, LANES:] = (w - hi.astype(F32)).astype(BF16)
    for b in range(3):
        wbc[b] = jnp.dot(wcat[b], e2_ref[...], preferred_element_type=F32)

    for r in range(N_PLANES):
        rows = slice(r * pr, (r + 1) * pr)
        o16 = jnp.concatenate([o16_ref[r, h] for h in range(N_HEADS)], axis=1)
        a = (wbc[0, rows, :] * o1_ref[r].astype(F32) + wbc[1, rows, :] * o4_ref[r].astype(F32)
             + wbc[2, rows, :] * o16)
        mixed[rows, :aw] = _rms_rows(a, ga_ref[...]).astype(BF16)
        mixed[rows, aw:] = _rms_rows(f_ref[r].astype(F32), gf_ref[...]).astype(BF16)

    ybuf[...] = jnp.dot(mixed[...], w_ref[...], preferred_element_type=F32)

    gate1 = mod_ref[2:3, :]
    shift2 = mod_ref[3:4, :]
    scale2 = 1.0 + mod_ref[4:5, :]
    for r in range(N_PLANES):
        rows = slice(r * pr, (r + 1) * pr)
        xr = jnp.concatenate([xslab[s, pl.ds(r, pr, stride=N_PLANES), :] for s in range(D_SLABS)], axis=1)
        x1 = xr + gate1 * _rms_rows(ybuf[rows, :], gpm_ref[...])
        x1_ref[r] = x1
        h2_ref[r] = (_rms_rows(x1, gpl_ref[...]) * scale2 + shift2).astype(BF16)


def _lane_broadcast_matrix():
    e = np.zeros((2 * LANES, ATTN_WIDTH), np.float32)
    for h in range(N_HEADS):
        e[h, h * HEAD_DIM:(h + 1) * HEAD_DIM] = 1.0
        e[LANES + h, h * HEAD_DIM:(h + 1) * HEAD_DIM] = 1.0
    return jnp.asarray(e, BF16)


def _out_proj(o1, o4, o16, l1, l4, l16, f, xg, mod, nseg, g_attn_out, g_fourier_out, g_post_mix, g_pre_mlp, w_out_bf16):
    G = xg.shape[0]
    tm = 512
    pr = tm // N_PLANES
    aw, fw = ATTN_WIDTH, FOURIER_WIDTH
    plane = lambda g, i: (g, 0, i, 0)
    const2 = lambda g, i: (0, 0)
    return pl.pallas_call(
        functools.partial(_out_proj_kernel, tm=tm),
        grid=(G, SEG // tm),
        in_specs=[
            pl.BlockSpec((None, N_PLANES, pr, aw), plane),
            pl.BlockSpec((None, N_PLANES, pr, aw), plane),
            pl.BlockSpec((None, N_PLANES, N_HEADS, pr, LANES), lambda g, i: (g, 0, 0, i, 0)),
            pl.BlockSpec((None, N_PLANES, pr, LANES), plane),
            pl.BlockSpec((None, N_PLANES, pr, LANES), plane),
            pl.BlockSpec((None, N_PLANES, pr, LANES), plane),
            pl.BlockSpec((None, N_PLANES, pr, fw), plane),
            pl.BlockSpec((None, tm, D_MODEL), lambda g, i: (g, i, 0)),
            pl.BlockSpec((None, N_MOD, D_MODEL), lambda g, i: (g // nseg, 0, 0)),
            pl.BlockSpec((1, aw), const2),
            pl.BlockSpec((1, fw), const2),
            pl.BlockSpec((1, D_MODEL), const2),
            pl.BlockSpec((1, D_MODEL), const2),
            pl.BlockSpec((2 * LANES, aw), const2, pipeline_mode=pl.Buffered(1)),
            pl.BlockSpec((aw + fw, D_MODEL), const2, pipeline_mode=pl.Buffered(1)),
        ],
        out_specs=[pl.BlockSpec((None, N_PLANES, pr, D_MODEL), plane)] * 2,
        out_shape=[jax.ShapeDtypeStruct((G, N_PLANES, PLANE_ROWS, D_MODEL), F32),
                   jax.ShapeDtypeStruct((G, N_PLANES, PLANE_ROWS, D_MODEL), BF16)],
        scratch_shapes=[
            pltpu.VMEM((D_SLABS, tm, LANES), F32),
            pltpu.VMEM((3, tm, 2 * LANES), BF16),
            pltpu.VMEM((3, tm, aw), F32),
            pltpu.VMEM((tm, aw + fw), BF16),
            pltpu.VMEM((tm, D_MODEL), F32),
        ],
        compiler_params=_params(("parallel", "parallel"), 60),
        name="out_proj",
    )(o1, o4, o16, l1, l4, l16, f, xg, mod, g_attn_out.reshape(1, aw), g_fourier_out.reshape(1, fw),
      g_post_mix.reshape(1, D_MODEL), g_pre_mlp.reshape(1, D_MODEL), _lane_broadcast_matrix(), w_out_bf16)


def _mlp_kernel(h_ref, x1_ref, mod_ref, g_ref, w1_ref, w2_ref, out_ref, acc_ref, oslab, *, tm, n_j):
    j = pl.program_id(2)
    pr = tm // N_PLANES

    @pl.when(j == 0)
    def _():
        acc_ref[...] = jnp.zeros_like(acc_ref)

    h = h_ref[...].reshape(tm, D_MODEL)
    hid = jnp.dot(h, w1_ref[...], preferred_element_type=F32)
    hid = jnp.square(jnp.maximum(hid, 0.0)).astype(BF16)
    acc_ref[...] += jnp.dot(hid, w2_ref[...], preferred_element_type=F32)

    @pl.when(j == n_j - 1)
    def _():
        gate2 = mod_ref[5:6, :]
        rc = 64
        for r in range(N_PLANES):
            def body(c, carry):
                m0 = pl.multiple_of(c * rc, rc)
                y = acc_ref[pl.ds(r * pr + m0, rc), :]
                v = x1_ref[r, pl.ds(m0, rc), :] + gate2 * _rms_rows(y, g_ref[...])
                for s in range(D_SLABS):
                    oslab[s, pl.ds(r + N_PLANES * m0, rc, stride=N_PLANES), :] = v[:, _lane_slab(s)]
                return carry

            lax.fori_loop(0, pr // rc, body, 0)
        for s in range(D_SLABS):
            out_ref[:, _lane_slab(s)] = oslab[s]


def _mlp(h2, x1, mod, nseg, g_post_mlp, w1_bf16, w2_bf16):
    G = x1.shape[0]
    tm, tf = 512, 1024
    pr = tm // N_PLANES
    n_j = D_FF // tf
    plane = lambda g, i, j: (g, 0, i, 0)
    return pl.pallas_call(
        functools.partial(_mlp_kernel, tm=tm, n_j=n_j),
        grid=(G, SEG // tm, n_j),
        in_specs=[
            pl.BlockSpec((None, N_PLANES, pr, D_MODEL), plane),
            pl.BlockSpec((None, N_PLANES, pr, D_MODEL), plane),
            pl.BlockSpec((None, N_MOD, D_MODEL), lambda g, i, j: (g // nseg, 0, 0)),
            pl.BlockSpec((1, D_MODEL), lambda g, i, j: (0, 0)),
            pl.BlockSpec((D_MODEL, tf), lambda g, i, j: (0, j)),
            pl.BlockSpec((tf, D_MODEL), lambda g, i, j: (j, 0)),
        ],
        out_specs=pl.BlockSpec((None, tm, D_MODEL), lambda g, i, j: (g, i, 0)),
        out_shape=jax.ShapeDtypeStruct((G, SEG, D_MODEL), F32),
        scratch_shapes=[pltpu.VMEM((tm, D_MODEL), F32), pltpu.VMEM((D_SLABS, tm, LANES), F32)],
        compiler_params=_params(("parallel", "parallel", "arbitrary"), 56),
        name="mlp",
    )(h2, x1, mod, g_post_mlp.reshape(1, D_MODEL), w1_bf16, w2_bf16)


def _layer(x, mod, p):
    B, S, _ = x.shape
    assert S % SEG == 0
    nseg = S // SEG
    G = B * nseg
    xg = x.reshape(G, SEG, D_MODEL)
    z4, z16 = _in_proj(xg, mod, nseg, p["g_pre_mix"], p["w_in"])
    z4s = z4.reshape(B, nseg, N_PLANES, PLANE_ROWS, IN_COLS)
    z16s = z16.reshape(B, nseg, 4, N_PLANES, CLASS_ROWS, 3 * ATTN_WIDTH)
    o1, l1 = _attn_d1(z4s, B, nseg)
    o4, l4 = _attn_d4(z4s, B, nseg)
    o16, l16 = _attn_d16(z16s, B, nseg)
    f = _fourier(z4s, p["wcs"], p["tables_nat"], p["tables_plane"], B, nseg)
    pshape = (G, N_PLANES, PLANE_ROWS)
    x1, h2 = _out_proj(
        o1.reshape(pshape + (ATTN_WIDTH,)), o4.reshape(pshape + (ATTN_WIDTH,)),
        o16.reshape((G, N_PLANES, N_HEADS, PLANE_ROWS, LANES)),
        l1.reshape(pshape + (LANES,)), l4.reshape(pshape + (LANES,)), l16.reshape(pshape + (LANES,)),
        f, xg, mod, nseg, p["g_attn_out"], p["g_fourier_out"], p["g_post_mix"], p["g_pre_mlp"], p["w_out"])
    out = _mlp(h2, x1, mod, nseg, p["g_post_mlp"], p["w_mlp_in"], p["w_mlp_out"])
    return out.reshape(B, S, D_MODEL)


def kernel(x_prompt, x_sample, c_prompt, c_sample, w_ada, b_ada, g_pre_mix, w_in, g_attn_out, w_fourier,
           g_fourier_out, w_out, g_post_mix, g_pre_mlp, w_mlp_in, w_mlp_out, g_post_mlp):
    depth = w_ada.shape[0]
    nb_p, nb_s = c_prompt.shape[0], c_sample.shape[0]
    pad = (-(nb_p + nb_s)) % 8
    c_all = jnp.concatenate([c_prompt, c_sample, jnp.zeros((pad, D_MODEL), F32)], axis=0)
    tables_nat, tables_plane = _dft_tables()
    xp, xs = x_prompt, x_sample
    for l in range(depth):
        mod = _modulation(c_all, w_ada[l], b_ada[l])
        mod_p = mod[:nb_p].reshape(nb_p, N_MOD, D_MODEL)
        mod_s = mod[nb_p:nb_p + nb_s].reshape(nb_s, N_MOD, D_MODEL)
        p = dict(
            g_pre_mix=g_pre_mix[l], w_in=w_in[l].astype(BF16), g_attn_out=g_attn_out[l],
            g_fourier_out=g_fourier_out[l], w_out=w_out[l].astype(BF16), g_post_mix=g_post_mix[l],
            g_pre_mlp=g_pre_mlp[l], w_mlp_in=w_mlp_in[l].astype(BF16), w_mlp_out=w_mlp_out[l].astype(BF16),
            g_post_mlp=g_post_mlp[l], wcs=_fold_channel_dft(w_fourier[l]),
            tables_nat=tables_nat, tables_plane=tables_plane,
        )
        xp = _layer(xp, mod_p, p)
        xs = _layer(xs, mod_s, p)
    return (xp, xs)
```

```python
import functools
import math

import numpy as np
import jax
import jax.numpy as jnp
from jax import lax
from jax.experimental import pallas as pl
from jax.experimental.pallas import tpu as pltpu

D_MODEL = 2048
ATTN_WIDTH = 1024
FOURIER_WIDTH = 1024
HEAD_DIM = 128
N_HEADS = 8
N_GROUPS = 4
GROUP_DIM = 256
D_FF = 8192
IN_COLS = 3 * ATTN_WIDTH + FOURIER_WIDTH
DILATED_BRANCHES = ((128, 1), (512, 4), (2048, 16))
HALF = 64
N_MOD = 6
RMS_EPS = 1e-6
NEG_INF = -1e30

SEG = 2048
N_PLANES = 4
PLANE_ROWS = SEG // N_PLANES
N_CLASSES = 16
CLASS_ROWS = SEG // N_CLASSES
LANES = 128
D_SLABS = D_MODEL // LANES
Q_SUB = 128
K_SUB = 256
K_PAD = K_SUB - (Q_SUB + 2 * HALF)

F32 = jnp.float32
BF16 = jnp.bfloat16
MIB = 1024 * 1024

assert DILATED_BRANCHES == ((128, 1), (512, 4), (2048, 16))
assert all(w // (2 * d) == HALF for w, d in DILATED_BRANCHES)


def _params(semantics, vmem_mib):
    return pltpu.CompilerParams(dimension_semantics=semantics, vmem_limit_bytes=vmem_mib * MIB)


def _rms_rows(v, gain):
    ms = jnp.mean(v * v, axis=-1, keepdims=True)
    return v * lax.rsqrt(ms + RMS_EPS) * gain


def _lane_slab(s):
    return slice(s * LANES, (s + 1) * LANES)


def _mod_kernel(c_ref, w_ref, b_ref, o_ref):
    c = c_ref[...]
    a = (c * jax.nn.sigmoid(c)).astype(BF16)
    o_ref[...] = jnp.dot(a, w_ref[...].astype(BF16), preferred_element_type=F32) + b_ref[...]


def _modulation(c_all, w_ada, b_ada):
    rows = c_all.shape[0]
    n = w_ada.shape[1]
    tn = 1024
    return pl.pallas_call(
        _mod_kernel,
        grid=(n // tn,),
        in_specs=[
            pl.BlockSpec((rows, D_MODEL), lambda j: (0, 0)),
            pl.BlockSpec((D_MODEL, tn), lambda j: (0, j)),
            pl.BlockSpec((1, tn), lambda j: (0, j)),
        ],
        out_specs=pl.BlockSpec((rows, tn), lambda j: (0, j)),
        out_shape=jax.ShapeDtypeStruct((rows, n), F32),
        compiler_params=_params(("parallel",), 40),
        name="mod",
    )(c_all, w_ada, b_ada.reshape(1, n))


def _in_proj_kernel(x_ref, mod_ref, g_ref, w_ref, z4_ref, z16_ref, h_ref, hslab, aslab, *, tm, tn, nb):
    j = pl.program_id(2)
    pr = tm // N_PLANES
    pb = nb // N_PLANES

    @pl.when(j == 0)
    def _():
        shift = mod_ref[0:1, :]
        scale1 = 1.0 + mod_ref[1:2, :]
        gain = g_ref[...]

        def block(c, carry):
            n0 = pl.multiple_of(c * nb, nb)
            rq = 16
            for q in range(nb // rq):
                h = _rms_rows(x_ref[pl.ds(n0 + q * rq, rq), :], gain) * scale1 + shift
                for s in range(D_SLABS):
                    hslab[s, q * rq:(q + 1) * rq, :] = h[:, _lane_slab(s)]
            m0 = pl.multiple_of(c * pb, pb)
            for r in range(N_PLANES):
                for s in range(D_SLABS):
                    h_ref[pl.ds(r * pr + m0, pb), _lane_slab(s)] = (
                        hslab[s, pl.ds(r, pb, stride=N_PLANES), :].astype(BF16))
            return carry

        lax.fori_loop(0, tm // nb, block, 0)

    acc = jnp.dot(h_ref[...], w_ref[...], preferred_element_type=F32)
    acc = acc * jnp.where(j == 1, HEAD_DIM ** -0.5, 1.0).astype(F32)
    for r in range(N_PLANES):
        z4_ref[r] = acc[r * pr:(r + 1) * pr].astype(BF16)
    for s in range(tn // LANES):
        aslab[s] = acc[:, _lane_slab(s)]
    cr = pr // 4
    for r in range(N_PLANES):
        for a in range(4):
            for s in range(tn // LANES):
                z16_ref[r + 4 * a, :, _lane_slab(s)] = (
                    aslab[s, pl.ds(r * pr + a, cr, stride=4), :].astype(BF16))


def _in_proj(xg, mod, nseg, g_pre_mix, w_in_bf16):
    G = xg.shape[0]
    tm, tn = 1024, 1024
    n_j = IN_COLS // tn
    return pl.pallas_call(
        functools.partial(_in_proj_kernel, tm=tm, tn=tn, nb=256),
        grid=(G, SEG // tm, n_j),
        in_specs=[
            pl.BlockSpec((None, tm, D_MODEL), lambda g, i, j: (g, i, 0)),
            pl.BlockSpec((None, N_MOD, D_MODEL), lambda g, i, j: (g // nseg, 0, 0)),
            pl.BlockSpec((1, D_MODEL), lambda g, i, j: (0, 0)),
            pl.BlockSpec((D_MODEL, tn), lambda g, i, j: (0, (j + n_j - 1) % n_j)),
        ],
        out_specs=[
            pl.BlockSpec((None, N_PLANES, tm // N_PLANES, tn), lambda g, i, j: (g, 0, i, (j + n_j - 1) % n_j)),
            pl.BlockSpec((None, N_CLASSES, tm // N_CLASSES, tn), lambda g, i, j: (g, 0, i, jnp.maximum(j, 1) - 1)),
        ],
        out_shape=[
            jax.ShapeDtypeStruct((G, N_PLANES, PLANE_ROWS, IN_COLS), BF16),
            jax.ShapeDtypeStruct((G, N_CLASSES, CLASS_ROWS, 3 * ATTN_WIDTH), BF16),
        ],
        scratch_shapes=[
            pltpu.VMEM((tm, D_MODEL), BF16),
            pltpu.VMEM((D_SLABS, 256, LANES), F32),
            pltpu.VMEM((tn // LANES, tm, LANES), F32),
        ],
        compiler_params=_params(("parallel", "parallel", "arbitrary"), 56),
        name="in_proj",
    )(xg, mod, g_pre_mix.reshape(1, D_MODEL), w_in_bf16)


def _softmax_pv(s, vh):
    m = jnp.max(s, axis=-1, keepdims=True)
    p = jnp.exp(s - m)
    den = jnp.sum(p, axis=-1, keepdims=True)
    o = jnp.dot(p.astype(BF16), vh, preferred_element_type=F32) / den
    return o, m + jnp.log(den)


def _scores(qh, kh, bias):
    return lax.dot_general(qh, kh, (((1,), (1,)), ((), ())), preferred_element_type=F32) + bias


def _alibi_tile(rel, dil):
    slopes = 2.0 ** (-8.0 * (np.arange(N_HEADS, dtype=np.float64) + 1.0) / N_HEADS)
    bias = -(slopes * dil)[:, None, None] * rel[None]
    return jnp.asarray(np.where(rel[None] <= HALF, bias, NEG_INF), dtype=F32)


def _band_bias(dil):
    rel = np.abs((np.arange(K_SUB)[None, :] - HALF) - np.arange(Q_SUB)[:, None]).astype(np.float64)
    return _alibi_tile(rel, dil)


def _plane_bias():
    qr, qm = np.divmod(np.arange(Q_SUB), Q_SUB // N_PLANES)
    kr, km = np.divmod(np.arange(K_SUB), K_SUB // N_PLANES)
    rel = np.abs(N_PLANES * (km[None, :] - HALF // N_PLANES - qm[:, None]) + (kr[None, :] - qr[:, None])).astype(np.float64)
    return _alibi_tile(rel, 1)


def _attn_class_kernel(q_ref, kp_ref, kc_ref, kn_ref, vp_ref, vc_ref, vn_ref, bias_ref,
                       o_ref, lse_ref, kext, vext, *, lq, n_i):
    i = pl.program_id(1)
    n_sub = lq // Q_SUB
    col = lax.broadcasted_iota(jnp.int32, (Q_SUB, K_SUB), 1)
    lane = lax.broadcasted_iota(jnp.int32, (Q_SUB, LANES), 1)

    if K_PAD:
        kext[2 * HALF + lq:, :] = jnp.zeros((K_PAD, ATTN_WIDTH), BF16)
        vext[2 * HALF + lq:, :] = jnp.zeros((K_PAD, ATTN_WIDTH), BF16)

    def plane(r, carry):
        kext[0:HALF, :] = kp_ref[r]
        kext[HALF:HALF + lq, :] = kc_ref[r]
        kext[HALF + lq:2 * HALF + lq, :] = kn_ref[r]
        vext[0:HALF, :] = vp_ref[r]
        vext[HALF:HALF + lq, :] = vc_ref[r]
        vext[HALF + lq:2 * HALF + lq, :] = vn_ref[r]
        for sb in range(n_sub):
            r0 = sb * Q_SUB
            lse_tile = jnp.zeros((Q_SUB, LANES), F32)
            for h in range(N_HEADS):
                hc = _lane_slab(h)
                s = _scores(q_ref[r, r0:r0 + Q_SUB, hc], kext[r0:r0 + K_SUB, hc], bias_ref[h])
                if sb == 0:
                    s = jnp.where((col >= HALF) | (i > 0), s, NEG_INF)
                if sb == n_sub - 1:
                    s = jnp.where((col < Q_SUB + HALF) | (i < n_i - 1), s, NEG_INF)
                o, lse = _softmax_pv(s, vext[r0:r0 + K_SUB, hc])
                o_ref[r, r0:r0 + Q_SUB, hc] = o.astype(BF16)
                lse_tile = jnp.where(lane == h, lse, lse_tile)
            lse_ref[r, r0:r0 + Q_SUB, :] = lse_tile
        return carry

    lax.fori_loop(0, N_PLANES, plane, 0)


def _attn_d4(z4, B, nseg):
    aw = ATTN_WIDTH
    lq = PLANE_ROWS
    hb = lq // HALF

    def cur(part):
        return pl.BlockSpec((None, None, N_PLANES, lq, aw), lambda b, i: (b, i, 0, 0, part))

    def prev(part):
        return pl.BlockSpec((None, None, N_PLANES, HALF, aw), lambda b, i: (b, jnp.maximum(i - 1, 0), 0, hb - 1, part))

    def nxt(part):
        return pl.BlockSpec((None, None, N_PLANES, HALF, aw), lambda b, i: (b, jnp.minimum(i + 1, nseg - 1), 0, 0, part))

    return pl.pallas_call(
        functools.partial(_attn_class_kernel, lq=lq, n_i=nseg),
        grid=(B, nseg),
        in_specs=[cur(0), prev(1), cur(1), nxt(1), prev(2), cur(2), nxt(2),
                  pl.BlockSpec((N_HEADS, Q_SUB, K_SUB), lambda b, i: (0, 0, 0))],
        out_specs=[
            pl.BlockSpec((None, None, N_PLANES, lq, aw), lambda b, i: (b, i, 0, 0, 0)),
            pl.BlockSpec((None, None, N_PLANES, lq, LANES), lambda b, i: (b, i, 0, 0, 0)),
        ],
        out_shape=[
            jax.ShapeDtypeStruct((B, nseg, N_PLANES, PLANE_ROWS, aw), BF16),
            jax.ShapeDtypeStruct((B, nseg, N_PLANES, PLANE_ROWS, LANES), F32),
        ],
        scratch_shapes=[pltpu.VMEM((lq + 2 * HALF + K_PAD, aw), BF16)] * 2,
        compiler_params=_params(("parallel", "arbitrary"), 52),
        name="attn_d4",
    )(z4, z4, z4, z4, z4, z4, z4, _band_bias(4))


def _attn_d16_kernel(q_ref, kp_ref, kc_ref, kn_ref, vp_ref, vc_ref, vn_ref, bias_ref, o_ref, lse_ref, *, n_i, n_pl):
    i = pl.program_id(2)
    col = lax.broadcasted_iota(jnp.int32, (Q_SUB, K_SUB), 1)
    lane = lax.broadcasted_iota(jnp.int32, (Q_SUB, LANES), 1)

    n_sub = CLASS_ROWS // Q_SUB
    pad = [jnp.zeros((K_PAD, HEAD_DIM), BF16)] if K_PAD else []

    def plane(r, carry):
        for a in range(4):
            for sb in range(n_sub):
                r0 = sb * Q_SUB
                lse_tile = jnp.zeros((Q_SUB, LANES), F32)
                for h in range(N_HEADS):
                    hc = _lane_slab(h)
                    kh = jnp.concatenate([kp_ref[a, r, :, hc], kc_ref[a, r, :, hc], kn_ref[a, r, :, hc]] + pad,
                                         axis=0)[r0:r0 + K_SUB]
                    vh = jnp.concatenate([vp_ref[a, r, :, hc], vc_ref[a, r, :, hc], vn_ref[a, r, :, hc]] + pad,
                                         axis=0)[r0:r0 + K_SUB]
                    s = _scores(q_ref[a, r, r0:r0 + Q_SUB, hc], kh, bias_ref[h])
                    if sb == 0:
                        s = jnp.where((col >= HALF) | (i > 0), s, NEG_INF)
                    if sb == n_sub - 1:
                        s = jnp.where((col < Q_SUB + HALF) | (i < n_i - 1), s, NEG_INF)
                    o, lse = _softmax_pv(s, vh)
                    o_ref[r, h, pl.ds(a + 4 * r0, Q_SUB, stride=4), :] = o
                    lse_tile = jnp.where(lane == h, lse, lse_tile)
                lse_ref[r, pl.ds(a + 4 * r0, Q_SUB, stride=4), :] = lse_tile
        return carry

    lax.fori_loop(0, n_pl, plane, 0)


def _attn_d16(z16, B, nseg):
    aw = ATTN_WIDTH
    lq = CLASS_ROWS
    hb = lq // HALF
    n_pl = 2

    def cur(part):
        return pl.BlockSpec((None, None, 4, n_pl, lq, aw), lambda b, r, i: (b, i, 0, r, 0, part))

    def prev(part):
        return pl.BlockSpec((None, None, 4, n_pl, HALF, aw),
                            lambda b, r, i: (b, jnp.maximum(i - 1, 0), 0, r, hb - 1, part))

    def nxt(part):
        return pl.BlockSpec((None, None, 4, n_pl, HALF, aw),
                            lambda b, r, i: (b, jnp.minimum(i + 1, nseg - 1), 0, r, 0, part))

    return pl.pallas_call(
        functools.partial(_attn_d16_kernel, n_i=nseg, n_pl=n_pl),
        grid=(B, N_PLANES // n_pl, nseg),
        in_specs=[cur(0), prev(1), cur(1), nxt(1), prev(2), cur(2), nxt(2),
                  pl.BlockSpec((N_HEADS, Q_SUB, K_SUB), lambda b, r, i: (0, 0, 0))],
        out_specs=[
            pl.BlockSpec((None, None, n_pl, N_HEADS, PLANE_ROWS, LANES), lambda b, r, i: (b, i, r, 0, 0, 0)),
            pl.BlockSpec((None, None, n_pl, PLANE_ROWS, LANES), lambda b, r, i: (b, i, r, 0, 0)),
        ],
        out_shape=[
            jax.ShapeDtypeStruct((B, nseg, N_PLANES, N_HEADS, PLANE_ROWS, LANES), F32),
            jax.ShapeDtypeStruct((B, nseg, N_PLANES, PLANE_ROWS, LANES), F32),
        ],
        compiler_params=_params(("parallel", "parallel", "arbitrary"), 40),
        name="attn_d16",
    )(z16, z16, z16, z16, z16, z16, z16, _band_bias(16))


def _attn_d1_kernel(q_ref, kp_ref, kc_ref, kn_ref, vp_ref, vc_ref, vn_ref, bias_ref,
                    o_ref, lse_ref, kext, vext, *, mq, n_t):
    t = pl.program_id(1)
    qm = Q_SUB // N_PLANES
    km = K_SUB // N_PLANES
    hm = HALF // N_PLANES
    pm = K_PAD // N_PLANES
    for r in range(N_PLANES):
        kext[r, 0:hm, :] = kp_ref[r]
        kext[r, hm:hm + mq, :] = kc_ref[r]
        kext[r, hm + mq:2 * hm + mq, :] = kn_ref[r]
        vext[r, 0:hm, :] = vp_ref[r]
        vext[r, hm:hm + mq, :] = vc_ref[r]
        vext[r, hm + mq:2 * hm + mq, :] = vn_ref[r]
        if pm:
            kext[r, 2 * hm + mq:, :] = jnp.zeros((pm, ATTN_WIDTH), BF16)
            vext[r, 2 * hm + mq:, :] = jnp.zeros((pm, ATTN_WIDTH), BF16)

    n_sub = mq // qm
    colm = lax.broadcasted_iota(jnp.int32, (Q_SUB, K_SUB), 1) & (km - 1)
    lane = lax.broadcasted_iota(jnp.int32, (Q_SUB, LANES), 1)
    for sb in range(n_sub):
        m1 = sb * qm
        lse_tile = jnp.zeros((Q_SUB, LANES), F32)
        for h in range(N_HEADS):
            hc = _lane_slab(h)
            qh = jnp.concatenate([q_ref[r, m1:m1 + qm, hc] for r in range(N_PLANES)], axis=0)
            kh = jnp.concatenate([kext[r, m1:m1 + km, hc] for r in range(N_PLANES)], axis=0)
            vh = jnp.concatenate([vext[r, m1:m1 + km, hc] for r in range(N_PLANES)], axis=0)
            s = _scores(qh, kh, bias_ref[h])
            if sb == 0:
                s = jnp.where((colm >= hm) | (t > 0), s, NEG_INF)
            if sb == n_sub - 1:
                s = jnp.where((colm < qm + hm) | (t < n_t - 1), s, NEG_INF)
            o, lse = _softmax_pv(s, vh)
            for r in range(N_PLANES):
                o_ref[r, m1:m1 + qm, hc] = o[r * qm:(r + 1) * qm].astype(BF16)
            lse_tile = jnp.where(lane == h, lse, lse_tile)
        for r in range(N_PLANES):
            lse_ref[r, m1:m1 + qm, :] = lse_tile[r * qm:(r + 1) * qm]


def _attn_d1(z4, B, nseg):
    aw = ATTN_WIDTH
    mq = 256
    hm = HALF // N_PLANES
    per_seg = PLANE_ROWS // mq
    hb_step = mq // hm
    hb_seg = PLANE_ROWS // hm
    n_t = nseg * per_seg

    def cur(part):
        return pl.BlockSpec((None, None, N_PLANES, mq, aw), lambda b, t: (b, t // per_seg, 0, t % per_seg, part))

    def prev(part):
        def imap(b, t):
            q = jnp.maximum(t * hb_step - 1, 0)
            return (b, q // hb_seg, 0, q % hb_seg, part)
        return pl.BlockSpec((None, None, N_PLANES, hm, aw), imap)

    def nxt(part):
        def imap(b, t):
            q = jnp.minimum((t + 1) * hb_step, nseg * hb_seg - 1)
            return (b, q // hb_seg, 0, q % hb_seg, part)
        return pl.BlockSpec((None, None, N_PLANES, hm, aw), imap)

    return pl.pallas_call(
        functools.partial(_attn_d1_kernel, mq=mq, n_t=n_t),
        grid=(B, n_t),
        in_specs=[cur(0), prev(1), cur(1), nxt(1), prev(2), cur(2), nxt(2),
                  pl.BlockSpec((N_HEADS, Q_SUB, K_SUB), lambda b, t: (0, 0, 0))],
        out_specs=[
            pl.BlockSpec((None, None, N_PLANES, mq, aw), lambda b, t: (b, t // per_seg, 0, t % per_seg, 0)),
            pl.BlockSpec((None, None, N_PLANES, mq, LANES), lambda b, t: (b, t // per_seg, 0, t % per_seg, 0)),
        ],
        out_shape=[
            jax.ShapeDtypeStruct((B, nseg, N_PLANES, PLANE_ROWS, aw), BF16),
            jax.ShapeDtypeStruct((B, nseg, N_PLANES, PLANE_ROWS, LANES), F32),
        ],
        scratch_shapes=[pltpu.VMEM((N_PLANES, mq + 2 * hm + K_PAD // N_PLANES, aw), BF16)] * 2,
        compiler_params=_params(("parallel", "arbitrary"), 32),
        name="attn_d1",
    )(z4, z4, z4, z4, z4, z4, z4, _plane_bias())


def _dft_tables():
    n = SEG
    cols = (N_PLANES * (np.arange(n) % PLANE_ROWS) + np.arange(n) // PLANE_ROWS).astype(np.float64)

    def angles(rows):
        ang = 2.0 * np.pi * ((rows[..., None] * cols) % n) / n
        return jnp.asarray(np.cos(ang), F32), jnp.asarray(np.sin(ang), F32)

    def product(ca, sa, cb, sb):
        cos = (ca * cb - sa * sb).reshape(n, n).astype(BF16)
        msin = (-(sa * cb + ca * sb)).reshape(n, n).astype(BF16)
        return cos, msin

    ca, sa = angles(32.0 * np.arange(n // 32))
    cb, sb = angles((N_PLANES * np.arange(8)[None, :] + np.arange(N_PLANES)[:, None]).astype(np.float64))
    plane_rows = product(ca[None, :, None, :], sa[None, :, None, :], cb[:, None, :, :], sb[:, None, :, :])
    ca, sa = angles(8.0 * np.arange(n // 8))
    cb, sb = angles(np.arange(8, dtype=np.float64))
    natural_rows = product(ca[:, None, :], sa[:, None, :], cb[None, :, :], sb[None, :, :])
    return plane_rows, natural_rows


def _fold_kernel(c_ref, s_ref, w_ref, o_ref):
    w = w_ref[...]
    o_ref[:, :GROUP_DIM] = jnp.dot(c_ref[...], w, preferred_element_type=F32,
                                   precision=lax.Precision.HIGHEST).astype(BF16)
    o_ref[:, GROUP_DIM:] = jnp.dot(s_ref[...], w, preferred_element_type=F32,
                                   precision=lax.Precision.HIGHEST).astype(BF16)


def _fold_channel_dft(w_fourier):
    n = GROUP_DIM
    idx = (np.arange(n)[:, None] * np.arange(n)[None, :]) % n
    ang = 2.0 * np.pi * idx / n
    cos_c = jnp.asarray(np.cos(ang), F32)
    sin_c = jnp.asarray(np.sin(ang), F32)
    return pl.pallas_call(
        _fold_kernel,
        grid=(N_GROUPS,),
        in_specs=[
            pl.BlockSpec((n, n), lambda g: (0, 0)),
            pl.BlockSpec((n, n), lambda g: (0, 0)),
            pl.BlockSpec((None, n, n), lambda g: (g, 0, 0)),
        ],
        out_specs=pl.BlockSpec((None, n, 2 * n), lambda g: (g, 0, 0)),
        out_shape=jax.ShapeDtypeStruct((N_GROUPS, n, 2 * n), BF16),
        compiler_params=_params(("parallel",), 16),
        name="fold_channel_dft",
    )(cos_c, sin_c, w_fourier)


def _channel_dft(u, w_ref):
    parts = [jnp.dot(u[:, g * GROUP_DIM:(g + 1) * GROUP_DIM], w_ref[g], preferred_element_type=F32)
             for g in range(N_GROUPS)]
    va = jnp.concatenate([p[:, :GROUP_DIM] for p in parts], axis=1)
    vb = jnp.concatenate([p[:, GROUP_DIM:] for p in parts], axis=1)
    return va, vb


def _seq_dft_kernel(c_ref, ms_ref, u_ref, w_ref, f_ref, va_ref, vb_ref, *, norm):
    i = pl.program_id(1)

    @pl.when(i == 0)
    def _():
        for r in range(N_PLANES):
            rows = slice(r * PLANE_ROWS, (r + 1) * PLANE_ROWS)
            va, vb = _channel_dft(u_ref[r], w_ref)
            va_ref[rows, :] = va.astype(BF16)
            vb_ref[rows, :] = vb.astype(BF16)

    y = (jnp.dot(c_ref[...], va_ref[...], preferred_element_type=F32)
         + jnp.dot(ms_ref[...], vb_ref[...], preferred_element_type=F32))
    f_ref[...] = (y * norm).astype(BF16)


def _seq_dft(z4, wcs, tables, B, norm):
    fw = FOURIER_WIDTH
    u_col = IN_COLS // fw - 1
    tm = SEG // 2
    m_spec = pl.BlockSpec((tm, SEG), lambda b, i: (i, 0))
    return pl.pallas_call(
        functools.partial(_seq_dft_kernel, norm=norm),
        grid=(B, SEG // tm),
        in_specs=[m_spec, m_spec,
                  pl.BlockSpec((None, None, N_PLANES, PLANE_ROWS, fw), lambda b, i: (b, 0, 0, 0, u_col)),
                  pl.BlockSpec((N_GROUPS, GROUP_DIM, 2 * GROUP_DIM), lambda b, i: (0, 0, 0))],
        out_specs=pl.BlockSpec((None, tm, fw), lambda b, i: (b, i, 0)),
        out_shape=jax.ShapeDtypeStruct((B, SEG, fw), BF16),
        scratch_shapes=[pltpu.VMEM((SEG, fw), BF16), pltpu.VMEM((SEG, fw), BF16)],
        compiler_params=_params(("parallel", "arbitrary"), 56),
        name="four_seq",
    )(tables[0], tables[1], z4, wcs)


def _dif_kernel(u_ref, w_ref, tw_ref, a_ref, b_ref):
    va, vb = zip(*[_channel_dft(u_ref[m], w_ref) for m in range(4)])
    tw = tw_ref[...]
    sums = (
        (va[0] + va[1] + va[2] + va[3], -(vb[0] + vb[1] + vb[2] + vb[3])),
        (va[0] - vb[1] - va[2] + vb[3], -vb[0] - va[1] + vb[2] + va[3]),
        (va[0] - va[1] + va[2] - va[3], -vb[0] + vb[1] - vb[2] + vb[3]),
        (va[0] + vb[1] - va[2] - vb[3], -vb[0] + va[1] + vb[2] - va[3]),
    )
    for s, (qr, qi) in enumerate(sums):
        if s == 0:
            zr, zi = qr, qi
        else:
            c, sn = tw[:, s:s + 1], tw[:, 4 + s:5 + s]
            zr, zi = qr * c + qi * sn, qi * c - qr * sn
        a_ref[s] = zr.astype(BF16)
        b_ref[s] = (-zi).astype(BF16)


def _dif_prepare(z4, wcs, B):
    fw = FOURIER_WIDTH
    u_col = IN_COLS // fw - 1
    pb = 128
    S = N_PLANES * SEG
    r, p = np.meshgrid(np.arange(N_PLANES), np.arange(PLANE_ROWS), indexing="ij")
    n_in = (N_PLANES * p + r).astype(np.float64)[..., None]
    ang = 2.0 * np.pi * ((n_in * np.arange(4)) % S) / S
    tw = np.zeros((N_PLANES, PLANE_ROWS, LANES), np.float32)
    tw[..., :4] = np.cos(ang)
    tw[..., 4:8] = np.sin(ang)
    o_spec = pl.BlockSpec((None, 4, None, pb, fw), lambda b, r, i: (b, 0, r, i, 0))
    return pl.pallas_call(
        _dif_kernel,
        grid=(B, N_PLANES, PLANE_ROWS // pb),
        in_specs=[pl.BlockSpec((None, 4, None, pb, fw), lambda b, r, i: (b, 0, r, i, u_col)),
                  pl.BlockSpec((N_GROUPS, GROUP_DIM, 2 * GROUP_DIM), lambda b, r, i: (0, 0, 0)),
                  pl.BlockSpec((None, pb, LANES), lambda b, r, i: (r, i, 0))],
        out_specs=[o_spec, o_spec],
        out_shape=[jax.ShapeDtypeStruct((B, 4, N_PLANES, PLANE_ROWS, fw), BF16)] * 2,
        compiler_params=_params(("parallel", "parallel", "parallel"), 48),
        name="four_dif",
    )(z4, wcs, jnp.asarray(tw))


def _class_dft_kernel(c_ref, ms_ref, a_ref, b_ref, f_ref, *, norm):
    a = a_ref[...].reshape(SEG, FOURIER_WIDTH)
    b = b_ref[...].reshape(SEG, FOURIER_WIDTH)
    y = jnp.dot(c_ref[...], a, preferred_element_type=F32) + jnp.dot(ms_ref[...], b, preferred_element_type=F32)
    y = (y * norm).astype(BF16)
    for g in range(f_ref.shape[0]):
        f_ref[g] = y[g * PLANE_ROWS:(g + 1) * PLANE_ROWS]


def _class_dft(a, b, tables, B, norm):
    fw = FOURIER_WIDTH
    tm = SEG // 2
    gs = tm // PLANE_ROWS
    m_spec = pl.BlockSpec((tm, SEG), lambda b, s, i: (i, 0))
    v_spec = pl.BlockSpec((None, None, N_PLANES, PLANE_ROWS, fw), lambda b, s, i: (b, s, 0, 0, 0))
    return pl.pallas_call(
        functools.partial(_class_dft_kernel, norm=norm),
        grid=(B, 4, SEG // tm),
        in_specs=[m_spec, m_spec, v_spec, v_spec],
        out_specs=pl.BlockSpec((None, gs, None, PLANE_ROWS, fw), lambda b, s, i: (b, i, s, 0, 0)),
        out_shape=jax.ShapeDtypeStruct((B, 4, N_PLANES, PLANE_ROWS, fw), BF16),
        compiler_params=_params(("parallel", "parallel", "arbitrary"), 48),
        name="four_class_dft",
    )(tables[0], tables[1], a, b)


def _fourier(z4s, wcs, tables_plane_rows, tables_natural_rows, B, nseg):
    G = B * nseg
    fw = FOURIER_WIDTH
    norm = 1.0 / math.sqrt(nseg * SEG * GROUP_DIM)
    if nseg == 1:
        f = _seq_dft(z4s, wcs, tables_plane_rows, B, norm)
    else:
        assert nseg == 4
        a, b = _dif_prepare(z4s, wcs, B)
        f = _class_dft(a, b, tables_natural_rows, B, norm)
    return f.reshape(G, N_PLANES, PLANE_ROWS, fw)


def _out_proj_kernel(o1_ref, o4_ref, o16_ref, l1_ref, l4_ref, l16_ref, f_ref, x_ref, mod_ref,
                     ga_ref, gf_ref, gpm_ref, gpl_ref, e2_ref, w_ref, x1_ref, h2_ref,
                     xslab, wcat, wbc, mixed, ybuf, *, tm):
    aw = ATTN_WIDTH
    pr = tm // N_PLANES
    for s in range(D_SLABS):
        xslab[s] = x_ref[:, _lane_slab(s)]

    for r in range(N_PLANES):
        rows = slice(r * pr, (r + 1) * pr)
        l1, l4, l16 = l1_ref[r], l4_ref[r], l16_ref[r]
        top = jnp.maximum(jnp.maximum(l1, l4), l16)
        e1, e4, e16 = jnp.exp(l1 - top), jnp.exp(l4 - top), jnp.exp(l16 - top)
        inv = 1.0 / (e1 + e4 + e16)
        for b, e in enumerate((e1, e4, e16)):
            w = e * inv
            hi = w.astype(BF16)
            wcat[b, rows, 0:LANES] = hi
            wcat[b, rows, LANES:] = (w - hi.astype(F32)).astype(BF16)
    for b in range(3):
        wbc[b] = jnp.dot(wcat[b], e2_ref[...], preferred_element_type=F32)

    for r in range(N_PLANES):
        rows = slice(r * pr, (r + 1) * pr)
        o16 = jnp.concatenate([o16_ref[r, h] for h in range(N_HEADS)], axis=1)
        a = (wbc[0, rows, :] * o1_ref[r].astype(F32) + wbc[1, rows, :] * o4_ref[r].astype(F32)
             + wbc[2, rows, :] * o16)
        mixed[rows, :aw] = _rms_rows(a, ga_ref[...]).astype(BF16)
        mixed[rows, aw:] = _rms_rows(f_ref[r].astype(F32), gf_ref[...]).astype(BF16)

    ybuf[...] = jnp.dot(mixed[...], w_ref[...], preferred_element_type=F32)

    gate1 = mod_ref[2:3, :]
    shift2 = mod_ref[3:4, :]
    scale2 = 1.0 + mod_ref[4:5, :]
    for r in range(N_PLANES):
        rows = slice(r * pr, (r + 1) * pr)
        xr = jnp.concatenate([xslab[s, pl.ds(r, pr, stride=N_PLANES), :] for s in range(D_SLABS)], axis=1)
        x1 = xr + gate1 * _rms_rows(ybuf[rows, :], gpm_ref[...])
        x1_ref[r] = x1
        h2_ref[r] = (_rms_rows(x1, gpl_ref[...]) * scale2 + shift2).astype(BF16)


def _lane_broadcast_matrix():
    e = np.zeros((2 * LANES, ATTN_WIDTH), np.float32)
    for h in range(N_HEADS):
        e[h, h * HEAD_DIM:(h + 1) * HEAD_DIM] = 1.0
        e[LANES + h, h * HEAD_DIM:(h + 1) * HEAD_DIM] = 1.0
    return jnp.asarray(e, BF16)


def _out_proj(o1, o4, o16, l1, l4, l16, f, xg, mod, nseg, g_attn_out, g_fourier_out, g_post_mix, g_pre_mlp, w_out_bf16):
    G = xg.shape[0]
    tm = 512
    pr = tm // N_PLANES
    aw, fw = ATTN_WIDTH, FOURIER_WIDTH
    plane = lambda g, i: (g, 0, i, 0)
    const2 = lambda g, i: (0, 0)
    return pl.pallas_call(
        functools.partial(_out_proj_kernel, tm=tm),
        grid=(G, SEG // tm),
        in_specs=[
            pl.BlockSpec((None, N_PLANES, pr, aw), plane),
            pl.BlockSpec((None, N_PLANES, pr, aw), plane),
            pl.BlockSpec((None, N_PLANES, N_HEADS, pr, LANES), lambda g, i: (g, 0, 0, i, 0)),
            pl.BlockSpec((None, N_PLANES, pr, LANES), plane),
            pl.BlockSpec((None, N_PLANES, pr, LANES), plane),
            pl.BlockSpec((None, N_PLANES, pr, LANES), plane),
            pl.BlockSpec((None, N_PLANES, pr, fw), plane),
            pl.BlockSpec((None, tm, D_MODEL), lambda g, i: (g, i, 0)),
            pl.BlockSpec((None, N_MOD, D_MODEL), lambda g, i: (g // nseg, 0, 0)),
            pl.BlockSpec((1, aw), const2),
            pl.BlockSpec((1, fw), const2),
            pl.BlockSpec((1, D_MODEL), const2),
            pl.BlockSpec((1, D_MODEL), const2),
            pl.BlockSpec((2 * LANES, aw), const2, pipeline_mode=pl.Buffered(1)),
            pl.BlockSpec((aw + fw, D_MODEL), const2, pipeline_mode=pl.Buffered(1)),
        ],
        out_specs=[pl.BlockSpec((None, N_PLANES, pr, D_MODEL), plane)] * 2,
        out_shape=[jax.ShapeDtypeStruct((G, N_PLANES, PLANE_ROWS, D_MODEL), F32),
                   jax.ShapeDtypeStruct((G, N_PLANES, PLANE_ROWS, D_MODEL), BF16)],
        scratch_shapes=[
            pltpu.VMEM((D_SLABS, tm, LANES), F32),
            pltpu.VMEM((3, tm, 2 * LANES), BF16),
            pltpu.VMEM((3, tm, aw), F32),
            pltpu.VMEM((tm, aw + fw), BF16),
            pltpu.VMEM((tm, D_MODEL), F32),
        ],
        compiler_params=_params(("parallel", "parallel"), 60),
        name="out_proj",
    )(o1, o4, o16, l1, l4, l16, f, xg, mod, g_attn_out.reshape(1, aw), g_fourier_out.reshape(1, fw),
      g_post_mix.reshape(1, D_MODEL), g_pre_mlp.reshape(1, D_MODEL), _lane_broadcast_matrix(), w_out_bf16)


def _mlp_kernel(h_ref, x1_ref, mod_ref, g_ref, w1_ref, w2_ref, out_ref, acc_ref, oslab, *, tm, n_j):
    j = pl.program_id(2)
    pr = tm // N_PLANES

    @pl.when((pl.program_id(0) == 0) & (pl.program_id(1) == 0) & (j == 0))
    def _():
        acc_ref[...] = jnp.zeros_like(acc_ref)

    h = h_ref[...].reshape(tm, D_MODEL)
    hid = jnp.dot(h, w1_ref[...], preferred_element_type=F32)
    hid = jnp.square(jnp.maximum(hid, 0.0)).astype(BF16)
    prev = jnp.where(j == 0, 0.0, acc_ref[...])
    acc_ref[...] = prev + jnp.dot(hid, w2_ref[...], preferred_element_type=F32)

    @pl.when(j == n_j - 1)
    def _():
        gate2 = mod_ref[5:6, :]
        rc = 64
        for r in range(N_PLANES):
            def body(c, carry):
                m0 = pl.multiple_of(c * rc, rc)
                y = acc_ref[pl.ds(r * pr + m0, rc), :]
                v = x1_ref[r, pl.ds(m0, rc), :] + gate2 * _rms_rows(y, g_ref[...])
                for s in range(D_SLABS):
                    oslab[s, pl.ds(r + N_PLANES * m0, rc, stride=N_PLANES), :] = v[:, _lane_slab(s)]
                return carry

            lax.fori_loop(0, pr // rc, body, 0)
        for s in range(D_SLABS):
            out_ref[:, _lane_slab(s)] = oslab[s]


def _mlp(h2, x1, mod, nseg, g_post_mlp, w1_bf16, w2_bf16):
    G = x1.shape[0]
    tm, tf = 512, 1024
    pr = tm // N_PLANES
    n_j = D_FF // tf
    plane = lambda g, i, j: (g, 0, i, 0)
    return pl.pallas_call(
        functools.partial(_mlp_kernel, tm=tm, n_j=n_j),
        grid=(G, SEG // tm, n_j),
        in_specs=[
            pl.BlockSpec((None, N_PLANES, pr, D_MODEL), plane),
            pl.BlockSpec((None, N_PLANES, pr, D_MODEL), plane),
            pl.BlockSpec((None, N_MOD, D_MODEL), lambda g, i, j: (g // nseg, 0, 0)),
            pl.BlockSpec((1, D_MODEL), lambda g, i, j: (0, 0)),
            pl.BlockSpec((D_MODEL, tf), lambda g, i, j: (0, j)),
            pl.BlockSpec((tf, D_MODEL), lambda g, i, j: (j, 0)),
        ],
        out_specs=pl.BlockSpec((None, tm, D_MODEL), lambda g, i, j: (g, i, 0)),
        out_shape=jax.ShapeDtypeStruct((G, SEG, D_MODEL), F32),
        scratch_shapes=[pltpu.VMEM((tm, D_MODEL), F32), pltpu.VMEM((D_SLABS, tm, LANES), F32)],
        compiler_params=_params(("parallel", "parallel", "arbitrary"), 56),
        name="mlp",
    )(h2, x1, mod, g_post_mlp.reshape(1, D_MODEL), w1_bf16, w2_bf16)


def _layer(x, mod, p):
    B, S, _ = x.shape
    assert S % SEG == 0
    nseg = S // SEG
    G = B * nseg
    xg = x.reshape(G, SEG, D_MODEL)
    z4, z16 = _in_proj(xg, mod, nseg, p["g_pre_mix"], p["w_in"])
    z4s = z4.reshape(B, nseg, N_PLANES, PLANE_ROWS, IN_COLS)
    z16s = z16.reshape(B, nseg, 4, N_PLANES, CLASS_ROWS, 3 * ATTN_WIDTH)
    o1, l1 = _attn_d1(z4s, B, nseg)
    o4, l4 = _attn_d4(z4s, B, nseg)
    o16, l16 = _attn_d16(z16s, B, nseg)
    f = _fourier(z4s, p["wcs"], p["tables_plane_rows"], p["tables_natural_rows"], B, nseg)
    pshape = (G, N_PLANES, PLANE_ROWS)
    x1, h2 = _out_proj(
        o1.reshape(pshape + (ATTN_WIDTH,)), o4.reshape(pshape + (ATTN_WIDTH,)),
        o16.reshape((G, N_PLANES, N_HEADS, PLANE_ROWS, LANES)),
        l1.reshape(pshape + (LANES,)), l4.reshape(pshape + (LANES,)), l16.reshape(pshape + (LANES,)),
        f, xg, mod, nseg, p["g_attn_out"], p["g_fourier_out"], p["g_post_mix"], p["g_pre_mlp"], p["w_out"])
    out = _mlp(h2, x1, mod, nseg, p["g_post_mlp"], p["w_mlp_in"], p["w_mlp_out"])
    return out.reshape(B, S, D_MODEL)


def kernel(x_prompt, x_sample, c_prompt, c_sample, w_ada, b_ada, g_pre_mix, w_in, g_attn_out, w_fourier,
           g_fourier_out, w_out, g_post_mix, g_pre_mlp, w_mlp_in, w_mlp_out, g_post_mlp):
    depth = w_ada.shape[0]
    nb_p, nb_s = c_prompt.shape[0], c_sample.shape[0]
    pad = (-(nb_p + nb_s)) % 8
    c_all = jnp.concatenate([c_prompt, c_sample, jnp.zeros((pad, D_MODEL), F32)], axis=0)
    tables_plane_rows, tables_natural_rows = _dft_tables()
    xp, xs = x_prompt, x_sample
    for l in range(depth):
        mod = _modulation(c_all, w_ada[l], b_ada[l])
        mod_p = mod[:nb_p].reshape(nb_p, N_MOD, D_MODEL)
        mod_s = mod[nb_p:nb_p + nb_s].reshape(nb_s, N_MOD, D_MODEL)
        p = dict(
            g_pre_mix=g_pre_mix[l], w_in=w_in[l].astype(BF16), g_attn_out=g_attn_out[l],
            g_fourier_out=g_fourier_out[l], w_out=w_out[l].astype(BF16), g_post_mix=g_post_mix[l],
            g_pre_mlp=g_pre_mlp[l], w_mlp_in=w_mlp_in[l].astype(BF16), w_mlp_out=w_mlp_out[l].astype(BF16),
            g_post_mlp=g_post_mlp[l], wcs=_fold_channel_dft(w_fourier[l]),
            tables_plane_rows=tables_plane_rows, tables_natural_rows=tables_natural_rows,
        )
        xp = _layer(xp, mod_p, p)
        xs = _layer(xs, mod_s, p)
    return (xp, xs)
```

```python
import functools
import math

import numpy as np
import jax
import jax.numpy as jnp
from jax import lax
from jax.experimental import pallas as pl
from jax.experimental.pallas import tpu as pltpu

D_MODEL = 2048
ATTN_WIDTH = 1024
FOURIER_WIDTH = 1024
HEAD_DIM = 128
N_HEADS = 8
N_GROUPS = 4
GROUP_DIM = 256
D_FF = 8192
IN_COLS = 3 * ATTN_WIDTH + FOURIER_WIDTH
DILATED_BRANCHES = ((128, 1), (512, 4), (2048, 16))
HALF = 64
N_MOD = 6
RMS_EPS = 1e-6
NEG_INF = -1e30

SEG = 2048
N_PLANES = 4
PLANE_ROWS = SEG // N_PLANES
N_CLASSES = 16
CLASS_ROWS = SEG // N_CLASSES
LANES = 128
D_SLABS = D_MODEL // LANES
Q_SUB = 128
K_SUB = 256
K_PAD = K_SUB - (Q_SUB + 2 * HALF)

F32 = jnp.float32
BF16 = jnp.bfloat16
MIB = 1024 * 1024

assert DILATED_BRANCHES == ((128, 1), (512, 4), (2048, 16))
assert all(w // (2 * d) == HALF for w, d in DILATED_BRANCHES)


def _params(semantics, vmem_mib):
    return pltpu.CompilerParams(dimension_semantics=semantics, vmem_limit_bytes=vmem_mib * MIB)


def _rms_rows(v, gain):
    ms = jnp.mean(v * v, axis=-1, keepdims=True)
    return v * lax.rsqrt(ms + RMS_EPS) * gain


def _lane_slab(s):
    return slice(s * LANES, (s + 1) * LANES)


def _mod_kernel(c_ref, w_ref, b_ref, o_ref):
    c = c_ref[...]
    a = (c * jax.nn.sigmoid(c)).astype(BF16)
    o_ref[...] = jnp.dot(a, w_ref[...].astype(BF16), preferred_element_type=F32) + b_ref[...]


def _modulation(c_all, w_ada, b_ada):
    rows = c_all.shape[0]
    n = w_ada.shape[1]
    tn = 1024
    return pl.pallas_call(
        _mod_kernel,
        grid=(n // tn,),
        in_specs=[
            pl.BlockSpec((rows, D_MODEL), lambda j: (0, 0)),
            pl.BlockSpec((D_MODEL, tn), lambda j: (0, j)),
            pl.BlockSpec((1, tn), lambda j: (0, j)),
        ],
        out_specs=pl.BlockSpec((rows, tn), lambda j: (0, j)),
        out_shape=jax.ShapeDtypeStruct((rows, n), F32),
        compiler_params=_params(("parallel",), 40),
        name="mod",
    )(c_all, w_ada, b_ada.reshape(1, n))


def _in_proj_kernel(x_ref, mod_ref, g_ref, w_ref, z4_ref, z16_ref, h_ref, hslab, aslab, *, tm, tn, nb):
    j = pl.program_id(2)
    pr = tm // N_PLANES
    pb = nb // N_PLANES

    @pl.when(j == 0)
    def _():
        shift = mod_ref[0:1, :]
        scale1 = 1.0 + mod_ref[1:2, :]
        gain = g_ref[...]

        def block(c, carry):
            n0 = pl.multiple_of(c * nb, nb)
            rq = 16
            for q in range(nb // rq):
                h = _rms_rows(x_ref[pl.ds(n0 + q * rq, rq), :], gain) * scale1 + shift
                for s in range(D_SLABS):
                    hslab[s, q * rq:(q + 1) * rq, :] = h[:, _lane_slab(s)]
            m0 = pl.multiple_of(c * pb, pb)
            for r in range(N_PLANES):
                for s in range(D_SLABS):
                    h_ref[pl.ds(r * pr + m0, pb), _lane_slab(s)] = (
                        hslab[s, pl.ds(r, pb, stride=N_PLANES), :].astype(BF16))
            return carry

        lax.fori_loop(0, tm // nb, block, 0)

    acc = jnp.dot(h_ref[...], w_ref[...], preferred_element_type=F32)
    acc = acc * jnp.where(j == 1, HEAD_DIM ** -0.5, 1.0).astype(F32)
    for r in range(N_PLANES):
        z4_ref[r] = acc[r * pr:(r + 1) * pr].astype(BF16)
    for s in range(tn // LANES):
        aslab[s] = acc[:, _lane_slab(s)]
    cr = pr // 4
    for r in range(N_PLANES):
        for a in range(4):
            for s in range(tn // LANES):
                z16_ref[r + 4 * a, :, _lane_slab(s)] = (
                    aslab[s, pl.ds(r * pr + a, cr, stride=4), :].astype(BF16))


def _in_proj(xg, mod, nseg, g_pre_mix, w_in_bf16):
    G = xg.shape[0]
    tm, tn = 1024, 1024
    n_j = IN_COLS // tn
    return pl.pallas_call(
        functools.partial(_in_proj_kernel, tm=tm, tn=tn, nb=256),
        grid=(G, SEG // tm, n_j),
        in_specs=[
            pl.BlockSpec((None, tm, D_MODEL), lambda g, i, j: (g, i, 0)),
            pl.BlockSpec((None, N_MOD, D_MODEL), lambda g, i, j: (g // nseg, 0, 0)),
            pl.BlockSpec((1, D_MODEL), lambda g, i, j: (0, 0)),
            pl.BlockSpec((D_MODEL, tn), lambda g, i, j: (0, (j + n_j - 1) % n_j)),
        ],
        out_specs=[
            pl.BlockSpec((None, N_PLANES, tm // N_PLANES, tn), lambda g, i, j: (g, 0, i, (j + n_j - 1) % n_j)),
            pl.BlockSpec((None, N_CLASSES, tm // N_CLASSES, tn), lambda g, i, j: (g, 0, i, jnp.maximum(j, 1) - 1)),
        ],
        out_shape=[
            jax.ShapeDtypeStruct((G, N_PLANES, PLANE_ROWS, IN_COLS), BF16),
            jax.ShapeDtypeStruct((G, N_CLASSES, CLASS_ROWS, 3 * ATTN_WIDTH), BF16),
        ],
        scratch_shapes=[
            pltpu.VMEM((tm, D_MODEL), BF16),
            pltpu.VMEM((D_SLABS, 256, LANES), F32),
            pltpu.VMEM((tn // LANES, tm, LANES), F32),
        ],
        compiler_params=_params(("parallel", "parallel", "arbitrary"), 56),
        name="in_proj",
    )(xg, mod, g_pre_mix.reshape(1, D_MODEL), w_in_bf16)


def _softmax_pv(s, vh):
    m = jnp.max(s, axis=-1, keepdims=True)
    p = jnp.exp(s - m)
    den = jnp.sum(p, axis=-1, keepdims=True)
    o = jnp.dot(p.astype(BF16), vh, preferred_element_type=F32) / den
    return o, m + jnp.log(den)


def _scores(qh, kh, bias):
    return lax.dot_general(qh, kh, (((1,), (1,)), ((), ())), preferred_element_type=F32) + bias


def _alibi_tile(rel, dil):
    slopes = 2.0 ** (-8.0 * (np.arange(N_HEADS, dtype=np.float64) + 1.0) / N_HEADS)
    bias = -(slopes * dil)[:, None, None] * rel[None]
    return jnp.asarray(np.where(rel[None] <= HALF, bias, NEG_INF), dtype=F32)


def _band_bias(dil):
    rel = np.abs((np.arange(K_SUB)[None, :] - HALF) - np.arange(Q_SUB)[:, None]).astype(np.float64)
    return _alibi_tile(rel, dil)


def _plane_bias():
    qr, qm = np.divmod(np.arange(Q_SUB), Q_SUB // N_PLANES)
    kr, km = np.divmod(np.arange(K_SUB), K_SUB // N_PLANES)
    rel = np.abs(N_PLANES * (km[None, :] - HALF // N_PLANES - qm[:, None]) + (kr[None, :] - qr[:, None])).astype(np.float64)
    return _alibi_tile(rel, 1)


def _attn_class_kernel(q_ref, kp_ref, kc_ref, kn_ref, vp_ref, vc_ref, vn_ref, bias_ref,
                       o_ref, lse_ref, kext, vext, *, lq, n_i):
    i = pl.program_id(1)
    n_sub = lq // Q_SUB
    col = lax.broadcasted_iota(jnp.int32, (Q_SUB, K_SUB), 1)
    lane = lax.broadcasted_iota(jnp.int32, (Q_SUB, LANES), 1)

    if K_PAD:
        kext[2 * HALF + lq:, :] = jnp.zeros((K_PAD, ATTN_WIDTH), BF16)
        vext[2 * HALF + lq:, :] = jnp.zeros((K_PAD, ATTN_WIDTH), BF16)

    def plane(r, carry):
        kext[0:HALF, :] = kp_ref[r]
        kext[HALF:HALF + lq, :] = kc_ref[r]
        kext[HALF + lq:2 * HALF + lq, :] = kn_ref[r]
        vext[0:HALF, :] = vp_ref[r]
        vext[HALF:HALF + lq, :] = vc_ref[r]
        vext[HALF + lq:2 * HALF + lq, :] = vn_ref[r]
        for sb in range(n_sub):
            r0 = sb * Q_SUB
            lse_tile = jnp.zeros((Q_SUB, LANES), F32)
            for h in range(N_HEADS):
                hc = _lane_slab(h)
                s = _scores(q_ref[r, r0:r0 + Q_SUB, hc], kext[r0:r0 + K_SUB, hc], bias_ref[h])
                if sb == 0:
                    s = jnp.where((col >= HALF) | (i > 0), s, NEG_INF)
                if sb == n_sub - 1:
                    s = jnp.where((col < Q_SUB + HALF) | (i < n_i - 1), s, NEG_INF)
                o, lse = _softmax_pv(s, vext[r0:r0 + K_SUB, hc])
                o_ref[r, r0:r0 + Q_SUB, hc] = o.astype(BF16)
                lse_tile = jnp.where(lane == h, lse, lse_tile)
            lse_ref[r, r0:r0 + Q_SUB, :] = lse_tile
        return carry

    lax.fori_loop(0, N_PLANES, plane, 0)


def _attn_d4(z4, B, nseg):
    aw = ATTN_WIDTH
    lq = PLANE_ROWS
    hb = lq // HALF

    def cur(part):
        return pl.BlockSpec((None, None, N_PLANES, lq, aw), lambda b, i: (b, i, 0, 0, part))

    def prev(part):
        return pl.BlockSpec((None, None, N_PLANES, HALF, aw), lambda b, i: (b, jnp.maximum(i - 1, 0), 0, hb - 1, part))

    def nxt(part):
        return pl.BlockSpec((None, None, N_PLANES, HALF, aw), lambda b, i: (b, jnp.minimum(i + 1, nseg - 1), 0, 0, part))

    return pl.pallas_call(
        functools.partial(_attn_class_kernel, lq=lq, n_i=nseg),
        grid=(B, nseg),
        in_specs=[cur(0), prev(1), cur(1), nxt(1), prev(2), cur(2), nxt(2),
                  pl.BlockSpec((N_HEADS, Q_SUB, K_SUB), lambda b, i: (0, 0, 0))],
        out_specs=[
            pl.BlockSpec((None, None, N_PLANES, lq, aw), lambda b, i: (b, i, 0, 0, 0)),
            pl.BlockSpec((None, None, N_PLANES, lq, LANES), lambda b, i: (b, i, 0, 0, 0)),
        ],
        out_shape=[
            jax.ShapeDtypeStruct((B, nseg, N_PLANES, PLANE_ROWS, aw), BF16),
            jax.ShapeDtypeStruct((B, nseg, N_PLANES, PLANE_ROWS, LANES), F32),
        ],
        scratch_shapes=[pltpu.VMEM((lq + 2 * HALF + K_PAD, aw), BF16)] * 2,
        compiler_params=_params(("parallel", "arbitrary"), 52),
        name="attn_d4",
    )(z4, z4, z4, z4, z4, z4, z4, _band_bias(4))


def _attn_d16_kernel(q_ref, kp_ref, kc_ref, kn_ref, vp_ref, vc_ref, vn_ref, bias_ref, o_ref, lse_ref, *, n_i, n_pl):
    i = pl.program_id(2)
    col = lax.broadcasted_iota(jnp.int32, (Q_SUB, K_SUB), 1)
    lane = lax.broadcasted_iota(jnp.int32, (Q_SUB, LANES), 1)

    n_sub = CLASS_ROWS // Q_SUB
    pad = [jnp.zeros((K_PAD, HEAD_DIM), BF16)] if K_PAD else []

    def plane(r, carry):
        for a in range(4):
            for sb in range(n_sub):
                r0 = sb * Q_SUB
                lse_tile = jnp.zeros((Q_SUB, LANES), F32)
                for h in range(N_HEADS):
                    hc = _lane_slab(h)
                    kh = jnp.concatenate([kp_ref[a, r, :, hc], kc_ref[a, r, :, hc], kn_ref[a, r, :, hc]] + pad,
                                         axis=0)[r0:r0 + K_SUB]
                    vh = jnp.concatenate([vp_ref[a, r, :, hc], vc_ref[a, r, :, hc], vn_ref[a, r, :, hc]] + pad,
                                         axis=0)[r0:r0 + K_SUB]
                    s = _scores(q_ref[a, r, r0:r0 + Q_SUB, hc], kh, bias_ref[h])
                    if sb == 0:
                        s = jnp.where((col >= HALF) | (i > 0), s, NEG_INF)
                    if sb == n_sub - 1:
                        s = jnp.where((col < Q_SUB + HALF) | (i < n_i - 1), s, NEG_INF)
                    o, lse = _softmax_pv(s, vh)
                    o_ref[r, h, pl.ds(a + 4 * r0, Q_SUB, stride=4), :] = o
                    lse_tile = jnp.where(lane == h, lse, lse_tile)
                lse_ref[r, pl.ds(a + 4 * r0, Q_SUB, stride=4), :] = lse_tile
        return carry

    lax.fori_loop(0, n_pl, plane, 0)


def _attn_d16(z16, B, nseg):
    aw = ATTN_WIDTH
    lq = CLASS_ROWS
    hb = lq // HALF
    n_pl = 2

    def cur(part):
        return pl.BlockSpec((None, None, 4, n_pl, lq, aw), lambda b, r, i: (b, i, 0, r, 0, part))

    def prev(part):
        return pl.BlockSpec((None, None, 4, n_pl, HALF, aw),
                            lambda b, r, i: (b, jnp.maximum(i - 1, 0), 0, r, hb - 1, part))

    def nxt(part):
        return pl.BlockSpec((None, None, 4, n_pl, HALF, aw),
                            lambda b, r, i: (b, jnp.minimum(i + 1, nseg - 1), 0, r, 0, part))

    return pl.pallas_call(
        functools.partial(_attn_d16_kernel, n_i=nseg, n_pl=n_pl),
        grid=(B, N_PLANES // n_pl, nseg),
        in_specs=[cur(0), prev(1), cur(1), nxt(1), prev(2), cur(2), nxt(2),
                  pl.BlockSpec((N_HEADS, Q_SUB, K_SUB), lambda b, r, i: (0, 0, 0))],
        out_specs=[
            pl.BlockSpec((None, None, n_pl, N_HEADS, PLANE_ROWS, LANES), lambda b, r, i: (b, i, r, 0, 0, 0)),
            pl.BlockSpec((None, None, n_pl, PLANE_ROWS, LANES), lambda b, r, i: (b, i, r, 0, 0)),
        ],
        out_shape=[
            jax.ShapeDtypeStruct((B, nseg, N_PLANES, N_HEADS, PLANE_ROWS, LANES), F32),
            jax.ShapeDtypeStruct((B, nseg, N_PLANES, PLANE_ROWS, LANES), F32),
        ],
        compiler_params=_params(("parallel", "parallel", "arbitrary"), 40),
        name="attn_d16",
    )(z16, z16, z16, z16, z16, z16, z16, _band_bias(16))


def _attn_d1_kernel(q_ref, kp_ref, kc_ref, kn_ref, vp_ref, vc_ref, vn_ref, bias_ref,
                    o_ref, lse_ref, kext, vext, *, mq, n_t):
    t = pl.program_id(1)
    qm = Q_SUB // N_PLANES
    km = K_SUB // N_PLANES
    hm = HALF // N_PLANES
    pm = K_PAD // N_PLANES
    for r in range(N_PLANES):
        kext[r, 0:hm, :] = kp_ref[r]
        kext[r, hm:hm + mq, :] = kc_ref[r]
        kext[r, hm + mq:2 * hm + mq, :] = kn_ref[r]
        vext[r, 0:hm, :] = vp_ref[r]
        vext[r, hm:hm + mq, :] = vc_ref[r]
        vext[r, hm + mq:2 * hm + mq, :] = vn_ref[r]
        if pm:
            kext[r, 2 * hm + mq:, :] = jnp.zeros((pm, ATTN_WIDTH), BF16)
            vext[r, 2 * hm + mq:, :] = jnp.zeros((pm, ATTN_WIDTH), BF16)

    n_sub = mq // qm
    colm = lax.broadcasted_iota(jnp.int32, (Q_SUB, K_SUB), 1) & (km - 1)
    lane = lax.broadcasted_iota(jnp.int32, (Q_SUB, LANES), 1)
    for sb in range(n_sub):
        m1 = sb * qm
        lse_tile = jnp.zeros((Q_SUB, LANES), F32)
        for h in range(N_HEADS):
            hc = _lane_slab(h)
            qh = jnp.concatenate([q_ref[r, m1:m1 + qm, hc] for r in range(N_PLANES)], axis=0)
            kh = jnp.concatenate([kext[r, m1:m1 + km, hc] for r in range(N_PLANES)], axis=0)
            vh = jnp.concatenate([vext[r, m1:m1 + km, hc] for r in range(N_PLANES)], axis=0)
            s = _scores(qh, kh, bias_ref[h])
            if sb == 0:
                s = jnp.where((colm >= hm) | (t > 0), s, NEG_INF)
            if sb == n_sub - 1:
                s = jnp.where((colm < qm + hm) | (t < n_t - 1), s, NEG_INF)
            o, lse = _softmax_pv(s, vh)
            for r in range(N_PLANES):
                o_ref[r, m1:m1 + qm, hc] = o[r * qm:(r + 1) * qm].astype(BF16)
            lse_tile = jnp.where(lane == h, lse, lse_tile)
        for r in range(N_PLANES):
            lse_ref[r, m1:m1 + qm, :] = lse_tile[r * qm:(r + 1) * qm]


def _attn_d1(z4, B, nseg):
    aw = ATTN_WIDTH
    mq = 256
    hm = HALF // N_PLANES
    per_seg = PLANE_ROWS // mq
    hb_step = mq // hm
    hb_seg = PLANE_ROWS // hm
    n_t = nseg * per_seg

    def cur(part):
        return pl.BlockSpec((None, None, N_PLANES, mq, aw), lambda b, t: (b, t // per_seg, 0, t % per_seg, part))

    def prev(part):
        def imap(b, t):
            q = jnp.maximum(t * hb_step - 1, 0)
            return (b, q // hb_seg, 0, q % hb_seg, part)
        return pl.BlockSpec((None, None, N_PLANES, hm, aw), imap)

    def nxt(part):
        def imap(b, t):
            q = jnp.minimum((t + 1) * hb_step, nseg * hb_seg - 1)
            return (b, q // hb_seg, 0, q % hb_seg, part)
        return pl.BlockSpec((None, None, N_PLANES, hm, aw), imap)

    return pl.pallas_call(
        functools.partial(_attn_d1_kernel, mq=mq, n_t=n_t),
        grid=(B, n_t),
        in_specs=[cur(0), prev(1), cur(1), nxt(1), prev(2), cur(2), nxt(2),
                  pl.BlockSpec((N_HEADS, Q_SUB, K_SUB), lambda b, t: (0, 0, 0))],
        out_specs=[
            pl.BlockSpec((None, None, N_PLANES, mq, aw), lambda b, t: (b, t // per_seg, 0, t % per_seg, 0)),
            pl.BlockSpec((None, None, N_PLANES, mq, LANES), lambda b, t: (b, t // per_seg, 0, t % per_seg, 0)),
        ],
        out_shape=[
            jax.ShapeDtypeStruct((B, nseg, N_PLANES, PLANE_ROWS, aw), BF16),
            jax.ShapeDtypeStruct((B, nseg, N_PLANES, PLANE_ROWS, LANES), F32),
        ],
        scratch_shapes=[pltpu.VMEM((N_PLANES, mq + 2 * hm + K_PAD // N_PLANES, aw), BF16)] * 2,
        compiler_params=_params(("parallel", "arbitrary"), 32),
        name="attn_d1",
    )(z4, z4, z4, z4, z4, z4, z4, _plane_bias())


def _dft_tables(n):
    quarter = n // N_PLANES
    cols = (N_PLANES * (np.arange(n) % quarter) + np.arange(n) // quarter).astype(np.float64)

    def angles(rows):
        ang = 2.0 * np.pi * ((rows[:, None] * cols) % n) / n
        return jnp.asarray(np.cos(ang), F32), jnp.asarray(np.sin(ang), F32)

    ca, sa = angles(8.0 * np.arange(n // 8))
    cb, sb = angles(np.arange(8, dtype=np.float64))
    ca, sa, cb, sb = ca[:, None, :], sa[:, None, :], cb[None, :, :], sb[None, :, :]
    cos = (ca * cb - sa * sb).reshape(n, n).astype(BF16)
    msin = (-(sa * cb + ca * sb)).reshape(n, n).astype(BF16)
    return cos, msin


def _fold_kernel(c_ref, s_ref, w_ref, o_ref):
    w = w_ref[...]
    o_ref[:, :GROUP_DIM] = jnp.dot(c_ref[...], w, preferred_element_type=F32,
                                   precision=lax.Precision.HIGHEST).astype(BF16)
    o_ref[:, GROUP_DIM:] = jnp.dot(s_ref[...], w, preferred_element_type=F32,
                                   precision=lax.Precision.HIGHEST).astype(BF16)


def _fold_channel_dft(w_fourier):
    n = GROUP_DIM
    idx = (np.arange(n)[:, None] * np.arange(n)[None, :]) % n
    ang = 2.0 * np.pi * idx / n
    cos_c = jnp.asarray(np.cos(ang), F32)
    sin_c = jnp.asarray(np.sin(ang), F32)
    return pl.pallas_call(
        _fold_kernel,
        grid=(N_GROUPS,),
        in_specs=[
            pl.BlockSpec((n, n), lambda g: (0, 0)),
            pl.BlockSpec((n, n), lambda g: (0, 0)),
            pl.BlockSpec((None, n, n), lambda g: (g, 0, 0)),
        ],
        out_specs=pl.BlockSpec((None, n, 2 * n), lambda g: (g, 0, 0)),
        out_shape=jax.ShapeDtypeStruct((N_GROUPS, n, 2 * n), BF16),
        compiler_params=_params(("parallel",), 16),
        name="fold_channel_dft",
    )(cos_c, sin_c, w_fourier)


def _channel_dft(u, w_ref):
    parts = [jnp.dot(u[:, g * GROUP_DIM:(g + 1) * GROUP_DIM], w_ref[g], preferred_element_type=F32)
             for g in range(N_GROUPS)]
    va = jnp.concatenate([p[:, :GROUP_DIM] for p in parts], axis=1)
    vb = jnp.concatenate([p[:, GROUP_DIM:] for p in parts], axis=1)
    return va, vb


def _dif_classes(va, vb, tw):
    sums = (
        (va[0] + va[1] + va[2] + va[3], -(vb[0] + vb[1] + vb[2] + vb[3])),
        (va[0] - vb[1] - va[2] + vb[3], -vb[0] - va[1] + vb[2] + va[3]),
        (va[0] - va[1] + va[2] - va[3], -vb[0] + vb[1] - vb[2] + vb[3]),
        (va[0] + vb[1] - va[2] - vb[3], -vb[0] + va[1] + vb[2] - va[3]),
    )
    for s, (qr, qi) in enumerate(sums):
        if s == 0:
            yield qr, -qi
        else:
            c, sn = tw[:, s:s + 1], tw[:, 4 + s:5 + s]
            yield qr * c + qi * sn, qr * sn - qi * c


def _twiddle_table(n_total, positions):
    ang = 2.0 * np.pi * ((positions.astype(np.float64)[..., None] * np.arange(4)) % n_total) / n_total
    tw = np.zeros(positions.shape + (LANES,), np.float32)
    tw[..., :4] = np.cos(ang)
    tw[..., 4:8] = np.sin(ang)
    return jnp.asarray(tw)


def _seq_dif_kernel(c_ref, ms_ref, u_ref, w_ref, tw_ref, f_ref, a_scr, b_scr, *, norm):
    qr = PLANE_ROWS // 4
    for r in range(N_PLANES):
        va, vb = _channel_dft(u_ref[r], w_ref)
        va = [va[m * qr:(m + 1) * qr] for m in range(4)]
        vb = [vb[m * qr:(m + 1) * qr] for m in range(4)]
        for s, (re, neg_im) in enumerate(_dif_classes(va, vb, tw_ref[r])):
            a_scr[s, r * qr:(r + 1) * qr, :] = re.astype(BF16)
            b_scr[s, r * qr:(r + 1) * qr, :] = neg_im.astype(BF16)
    for s in range(4):
        y = (jnp.dot(c_ref[...], a_scr[s], preferred_element_type=F32)
             + jnp.dot(ms_ref[...], b_scr[s], preferred_element_type=F32))
        f_ref[s] = (y * norm).astype(BF16)


def _seq_dif(z4, wcs, tables, B, norm):
    fw = FOURIER_WIDTH
    u_col = IN_COLS // fw - 1
    nq = SEG // 4
    qr = PLANE_ROWS // 4
    r, p = np.meshgrid(np.arange(N_PLANES), np.arange(qr), indexing="ij")
    tw = _twiddle_table(SEG, N_PLANES * p + r)
    m_spec = pl.BlockSpec((nq, nq), lambda b: (0, 0))
    return pl.pallas_call(
        functools.partial(_seq_dif_kernel, norm=norm),
        grid=(B,),
        in_specs=[m_spec, m_spec,
                  pl.BlockSpec((None, None, N_PLANES, PLANE_ROWS, fw), lambda b: (b, 0, 0, 0, u_col)),
                  pl.BlockSpec((N_GROUPS, GROUP_DIM, 2 * GROUP_DIM), lambda b: (0, 0, 0)),
                  pl.BlockSpec((N_PLANES, qr, LANES), lambda b: (0, 0, 0))],
        out_specs=pl.BlockSpec((None, N_PLANES, PLANE_ROWS, fw), lambda b: (b, 0, 0, 0)),
        out_shape=jax.ShapeDtypeStruct((B, N_PLANES, PLANE_ROWS, fw), BF16),
        scratch_shapes=[pltpu.VMEM((4, nq, fw), BF16), pltpu.VMEM((4, nq, fw), BF16)],
        compiler_params=_params(("parallel",), 48),
        name="four_seq_dif",
    )(tables[0], tables[1], z4, wcs, tw)


def _dif_kernel(u_ref, w_ref, tw_ref, a_ref, b_ref):
    va, vb = zip(*[_channel_dft(u_ref[m], w_ref) for m in range(4)])
    for s, (re, neg_im) in enumerate(_dif_classes(va, vb, tw_ref[...])):
        a_ref[s] = re.astype(BF16)
        b_ref[s] = neg_im.astype(BF16)


def _dif_prepare(z4, wcs, B):
    fw = FOURIER_WIDTH
    u_col = IN_COLS // fw - 1
    pb = 128
    r, p = np.meshgrid(np.arange(N_PLANES), np.arange(PLANE_ROWS), indexing="ij")
    tw = _twiddle_table(N_PLANES * SEG, N_PLANES * p + r)
    o_spec = pl.BlockSpec((None, 4, None, pb, fw), lambda b, r, i: (b, 0, r, i, 0))
    return pl.pallas_call(
        _dif_kernel,
        grid=(B, N_PLANES, PLANE_ROWS // pb),
        in_specs=[pl.BlockSpec((None, 4, None, pb, fw), lambda b, r, i: (b, 0, r, i, u_col)),
                  pl.BlockSpec((N_GROUPS, GROUP_DIM, 2 * GROUP_DIM), lambda b, r, i: (0, 0, 0)),
                  pl.BlockSpec((None, pb, LANES), lambda b, r, i: (r, i, 0))],
        out_specs=[o_spec, o_spec],
        out_shape=[jax.ShapeDtypeStruct((B, 4, N_PLANES, PLANE_ROWS, fw), BF16)] * 2,
        compiler_params=_params(("parallel", "parallel", "parallel"), 48),
        name="four_dif",
    )(z4, wcs, tw)


def _class_dft_kernel(c_ref, ms_ref, a_ref, b_ref, f_ref, *, norm):
    a = a_ref[...].reshape(SEG, FOURIER_WIDTH)
    b = b_ref[...].reshape(SEG, FOURIER_WIDTH)
    y = jnp.dot(c_ref[...], a, preferred_element_type=F32) + jnp.dot(ms_ref[...], b, preferred_element_type=F32)
    y = (y * norm).astype(BF16)
    for g in range(f_ref.shape[0]):
        f_ref[g] = y[g * PLANE_ROWS:(g + 1) * PLANE_ROWS]


def _class_dft(a, b, tables, B, norm):
    fw = FOURIER_WIDTH
    tm = SEG // 2
    gs = tm // PLANE_ROWS
    m_spec = pl.BlockSpec((tm, SEG), lambda b, s, i: (i, 0))
    v_spec = pl.BlockSpec((None, None, N_PLANES, PLANE_ROWS, fw), lambda b, s, i: (b, s, 0, 0, 0))
    return pl.pallas_call(
        functools.partial(_class_dft_kernel, norm=norm),
        grid=(B, 4, SEG // tm),
        in_specs=[m_spec, m_spec, v_spec, v_spec],
        out_specs=pl.BlockSpec((None, gs, None, PLANE_ROWS, fw), lambda b, s, i: (b, i, s, 0, 0)),
        out_shape=jax.ShapeDtypeStruct((B, 4, N_PLANES, PLANE_ROWS, fw), BF16),
        compiler_params=_params(("parallel", "parallel", "arbitrary"), 48),
        name="four_class_dft",
    )(tables[0], tables[1], a, b)


def _fourier(z4s, wcs, tables_quarter, tables_seg, B, nseg):
    G = B * nseg
    fw = FOURIER_WIDTH
    norm = 1.0 / math.sqrt(nseg * SEG * GROUP_DIM)
    if nseg == 1:
        f = _seq_dif(z4s, wcs, tables_quarter, B, norm)
    else:
        assert nseg == 4
        a, b = _dif_prepare(z4s, wcs, B)
        f = _class_dft(a, b, tables_seg, B, norm)
    return f.reshape(G, N_PLANES, PLANE_ROWS, fw)


def _out_proj_kernel(o1_ref, o4_ref, o16_ref, l1_ref, l4_ref, l16_ref, f_ref, x_ref, mod_ref,
                     ga_ref, gf_ref, gpm_ref, gpl_ref, e2_ref, w_ref, x1_ref, h2_ref,
                     xslab, wcat, wbc, mixed, ybuf, *, tm):
    aw = ATTN_WIDTH
    pr = tm // N_PLANES
    for s in range(D_SLABS):
        xslab[s] = x_ref[:, _lane_slab(s)]

    for r in range(N_PLANES):
        rows = slice(r * pr, (r + 1) * pr)
        l1, l4, l16 = l1_ref[r], l4_ref[r], l16_ref[r]
        top = jnp.maximum(jnp.maximum(l1, l4), l16)
        e1, e4, e16 = jnp.exp(l1 - top), jnp.exp(l4 - top), jnp.exp(l16 - top)
        inv = 1.0 / (e1 + e4 + e16)
        for b, e in enumerate((e1, e4, e16)):
            w = e * inv
            hi = w.astype(BF16)
            wcat[b, rows, 0:LANES] = hi
            wcat[b, rows, LANES:] = (w - hi.astype(F32)).astype(BF16)
    for b in range(3):
        wbc[b] = jnp.dot(wcat[b], e2_ref[...], preferred_element_type=F32)

    for r in range(N_PLANES):
        rows = slice(r * pr, (r + 1) * pr)
        o16 = jnp.concatenate([o16_ref[r, h] for h in range(N_HEADS)], axis=1)
        a = (wbc[0, rows, :] * o1_ref[r].astype(F32) + wbc[1, rows, :] * o4_ref[r].astype(F32)
             + wbc[2, rows, :] * o16)
        mixed[rows, :aw] = _rms_rows(a, ga_ref[...]).astype(BF16)
        mixed[rows, aw:] = _rms_rows(f_ref[r].astype(F32), gf_ref[...]).astype(BF16)

    ybuf[...] = jnp.dot(mixed[...], w_ref[...], preferred_element_type=F32)

    gate1 = mod_ref[2:3, :]
    shift2 = mod_ref[3:4, :]
    scale2 = 1.0 + mod_ref[4:5, :]
    for r in range(N_PLANES):
        rows = slice(r * pr, (r + 1) * pr)
        xr = jnp.concatenate([xslab[s, pl.ds(r, pr, stride=N_PLANES), :] for s in range(D_SLABS)], axis=1)
        x1 = xr + gate1 * _rms_rows(ybuf[rows, :], gpm_ref[...])
        x1_ref[r] = x1
        h2_ref[r] = (_rms_rows(x1, gpl_ref[...]) * scale2 + shift2).astype(BF16)


def _lane_broadcast_matrix():
    e = np.zeros((2 * LANES, ATTN_WIDTH), np.float32)
    for h in range(N_HEADS):
        e[h, h * HEAD_DIM:(h + 1) * HEAD_DIM] = 1.0
        e[LANES + h, h * HEAD_DIM:(h + 1) * HEAD_DIM] = 1.0
    return jnp.asarray(e, BF16)


def _out_proj(o1, o4, o16, l1, l4, l16, f, xg, mod, nseg, g_attn_out, g_fourier_out, g_post_mix, g_pre_mlp, w_out_bf16):
    G = xg.shape[0]
    tm = 512
    pr = tm // N_PLANES
    aw, fw = ATTN_WIDTH, FOURIER_WIDTH
    plane = lambda g, i: (g, 0, i, 0)
    const2 = lambda g, i: (0, 0)
    return pl.pallas_call(
        functools.partial(_out_proj_kernel, tm=tm),
        grid=(G, SEG // tm),
        in_specs=[
            pl.BlockSpec((None, N_PLANES, pr, aw), plane),
            pl.BlockSpec((None, N_PLANES, pr, aw), plane),
            pl.BlockSpec((None, N_PLANES, N_HEADS, pr, LANES), lambda g, i: (g, 0, 0, i, 0)),
            pl.BlockSpec((None, N_PLANES, pr, LANES), plane),
            pl.BlockSpec((None, N_PLANES, pr, LANES), plane),
            pl.BlockSpec((None, N_PLANES, pr, LANES), plane),
            pl.BlockSpec((None, N_PLANES, pr, fw), plane),
            pl.BlockSpec((None, tm, D_MODEL), lambda g, i: (g, i, 0)),
            pl.BlockSpec((None, N_MOD, D_MODEL), lambda g, i: (g // nseg, 0, 0)),
            pl.BlockSpec((1, aw), const2),
            pl.BlockSpec((1, fw), const2),
            pl.BlockSpec((1, D_MODEL), const2),
            pl.BlockSpec((1, D_MODEL), const2),
            pl.BlockSpec((2 * LANES, aw), const2, pipeline_mode=pl.Buffered(1)),
            pl.BlockSpec((aw + fw, D_MODEL), const2, pipeline_mode=pl.Buffered(1)),
        ],
        out_specs=[pl.BlockSpec((None, N_PLANES, pr, D_MODEL), plane)] * 2,
        out_shape=[jax.ShapeDtypeStruct((G, N_PLANES, PLANE_ROWS, D_MODEL), F32),
                   jax.ShapeDtypeStruct((G, N_PLANES, PLANE_ROWS, D_MODEL), BF16)],
        scratch_shapes=[
            pltpu.VMEM((D_SLABS, tm, LANES), F32),
            pltpu.VMEM((3, tm, 2 * LANES), BF16),
            pltpu.VMEM((3, tm, aw), F32),
            pltpu.VMEM((tm, aw + fw), BF16),
            pltpu.VMEM((tm, D_MODEL), F32),
        ],
        compiler_params=_params(("parallel", "parallel"), 60),
        name="out_proj",
    )(o1, o4, o16, l1, l4, l16, f, xg, mod, g_attn_out.reshape(1, aw), g_fourier_out.reshape(1, fw),
      g_post_mix.reshape(1, D_MODEL), g_pre_mlp.reshape(1, D_MODEL), _lane_broadcast_matrix(), w_out_bf16)


def _mlp_kernel(h_ref, x1_ref, mod_ref, g_ref, w1_ref, w2_ref, out_ref, acc_ref, oslab, *, tm, n_j):
    j = pl.program_id(2)
    pr = tm // N_PLANES

    @pl.when((pl.program_id(0) == 0) & (pl.program_id(1) == 0) & (j == 0))
    def _():
        acc_ref[...] = jnp.zeros_like(acc_ref)

    h = h_ref[...].reshape(tm, D_MODEL)
    hid = jnp.dot(h, w1_ref[...], preferred_element_type=F32)
    hid = jnp.square(jnp.maximum(hid, 0.0)).astype(BF16)
    prev = jnp.where(j == 0, 0.0, acc_ref[...])
    acc_ref[...] = prev + jnp.dot(hid, w2_ref[...], preferred_element_type=F32)

    @pl.when(j == n_j - 1)
    def _():
        gate2 = mod_ref[5:6, :]
        rc = 64
        for r in range(N_PLANES):
            def body(c, carry):
                m0 = pl.multiple_of(c * rc, rc)
                y = acc_ref[pl.ds(r * pr + m0, rc), :]
                v = x1_ref[r, pl.ds(m0, rc), :] + gate2 * _rms_rows(y, g_ref[...])
                for s in range(D_SLABS):
                    oslab[s, pl.ds(r + N_PLANES * m0, rc, stride=N_PLANES), :] = v[:, _lane_slab(s)]
                return carry

            lax.fori_loop(0, pr // rc, body, 0)
        for s in range(D_SLABS):
            out_ref[:, _lane_slab(s)] = oslab[s]


def _mlp(h2, x1, mod, nseg, g_post_mlp, w1_bf16, w2_bf16):
    G = x1.shape[0]
    tm, tf = 512, 1024
    pr = tm // N_PLANES
    n_j = D_FF // tf
    plane = lambda g, i, j: (g, 0, i, 0)
    return pl.pallas_call(
        functools.partial(_mlp_kernel, tm=tm, n_j=n_j),
        grid=(G, SEG // tm, n_j),
        in_specs=[
            pl.BlockSpec((None, N_PLANES, pr, D_MODEL), plane),
            pl.BlockSpec((None, N_PLANES, pr, D_MODEL), plane),
            pl.BlockSpec((None, N_MOD, D_MODEL), lambda g, i, j: (g // nseg, 0, 0)),
            pl.BlockSpec((1, D_MODEL), lambda g, i, j: (0, 0)),
            pl.BlockSpec((D_MODEL, tf), lambda g, i, j: (0, j)),
            pl.BlockSpec((tf, D_MODEL), lambda g, i, j: (j, 0)),
        ],
        out_specs=pl.BlockSpec((None, tm, D_MODEL), lambda g, i, j: (g, i, 0)),
        out_shape=jax.ShapeDtypeStruct((G, SEG, D_MODEL), F32),
        scratch_shapes=[pltpu.VMEM((tm, D_MODEL), F32), pltpu.VMEM((D_SLABS, tm, LANES), F32)],
        compiler_params=_params(("parallel", "parallel", "arbitrary"), 56),
        name="mlp",
    )(h2, x1, mod, g_post_mlp.reshape(1, D_MODEL), w1_bf16, w2_bf16)


def _layer(x, mod, p):
    B, S, _ = x.shape
    assert S % SEG == 0
    nseg = S // SEG
    G = B * nseg
    xg = x.reshape(G, SEG, D_MODEL)
    z4, z16 = _in_proj(xg, mod, nseg, p["g_pre_mix"], p["w_in"])
    z4s = z4.reshape(B, nseg, N_PLANES, PLANE_ROWS, IN_COLS)
    z16s = z16.reshape(B, nseg, 4, N_PLANES, CLASS_ROWS, 3 * ATTN_WIDTH)
    o1, l1 = _attn_d1(z4s, B, nseg)
    o4, l4 = _attn_d4(z4s, B, nseg)
    o16, l16 = _attn_d16(z16s, B, nseg)
    f = _fourier(z4s, p["wcs"], p["tables_quarter"], p["tables_seg"], B, nseg)
    pshape = (G, N_PLANES, PLANE_ROWS)
    x1, h2 = _out_proj(
        o1.reshape(pshape + (ATTN_WIDTH,)), o4.reshape(pshape + (ATTN_WIDTH,)),
        o16.reshape((G, N_PLANES, N_HEADS, PLANE_ROWS, LANES)),
        l1.reshape(pshape + (LANES,)), l4.reshape(pshape + (LANES,)), l16.reshape(pshape + (LANES,)),
        f, xg, mod, nseg, p["g_attn_out"], p["g_fourier_out"], p["g_post_mix"], p["g_pre_mlp"], p["w_out"])
    out = _mlp(h2, x1, mod, nseg, p["g_post_mlp"], p["w_mlp_in"], p["w_mlp_out"])
    return out.reshape(B, S, D_MODEL)


def kernel(x_prompt, x_sample, c_prompt, c_sample, w_ada, b_ada, g_pre_mix, w_in, g_attn_out, w_fourier,
           g_fourier_out, w_out, g_post_mix, g_pre_mlp, w_mlp_in, w_mlp_out, g_post_mlp):
    depth = w_ada.shape[0]
    nb_p, nb_s = c_prompt.shape[0], c_sample.shape[0]
    pad = (-(nb_p + nb_s)) % 8
    c_all = jnp.concatenate([c_prompt, c_sample, jnp.zeros((pad, D_MODEL), F32)], axis=0)
    tables_quarter, tables_seg = _dft_tables(SEG // 4), _dft_tables(SEG)
    xp, xs = x_prompt, x_sample
    for l in range(depth):
        mod = _modulation(c_all, w_ada[l], b_ada[l])
        mod_p = mod[:nb_p].reshape(nb_p, N_MOD, D_MODEL)
        mod_s = mod[nb_p:nb_p + nb_s].reshape(nb_s, N_MOD, D_MODEL)
        p = dict(
            g_pre_mix=g_pre_mix[l], w_in=w_in[l].astype(BF16), g_attn_out=g_attn_out[l],
            g_fourier_out=g_fourier_out[l], w_out=w_out[l].astype(BF16), g_post_mix=g_post_mix[l],
            g_pre_mlp=g_pre_mlp[l], w_mlp_in=w_mlp_in[l].astype(BF16), w_mlp_out=w_mlp_out[l].astype(BF16),
            g_post_mlp=g_post_mlp[l], wcs=_fold_channel_dft(w_fourier[l]),
            tables_quarter=tables_quarter, tables_seg=tables_seg,
        )
        xp = _layer(xp, mod_p, p)
        xs = _layer(xs, mod_s, p)
    return (xp, xs)
```
